```python
import functools
import jax, jax.numpy as jnp
from jax import lax
import numpy as np

D_MODEL = 2048
BATCH = 2
SEQ = 4096
DEPTH = 2
DEC_BATCH = 128
DEC_SEQ = 4
PAST_LEN = 2048
PAGE_SIZE = 128

A_WIDTH = D_MODEL // 4
A_CONV = 3
SB_HEAD_DIM = 128
SB_HEADS = D_MODEL // 256
SB_WIDTH = SB_HEADS * SB_HEAD_DIM
Q_BLOCK = 128
SB_BIAS_INIT = -6.0
C_WIDTH = D_MODEL // 4
C_CONV = 31
N_BRANCH = 3
N_GROUPS = 4
EXPERTS_PER_GROUP = 4
N_EXPERTS = N_GROUPS * EXPERTS_PER_GROUP
TOP_K = 2
D_EXPERT = D_MODEL // 4
RMS_EPS = 1e-6
LN_EPS = 1e-5
SPLIT_SIZES = (A_WIDTH, A_WIDTH, A_WIDTH, SB_WIDTH, SB_WIDTH, SB_WIDTH, C_WIDTH, C_WIDTH, N_BRANCH * D_MODEL)
IN_COLS = 3 * A_WIDTH + 3 * SB_WIDTH + 2 * C_WIDTH + N_BRANCH * D_MODEL

kernel_name = 'hybrid_gated_sbattn_conv_hmoe_step'


def rmsnorm(x, g):
    xf = x.astype(jnp.float32)
    y = xf * lax.rsqrt(jnp.mean(xf * xf, axis=-1, keepdims=True) + RMS_EPS)
    return (y * g.astype(jnp.float32)).astype(x.dtype)


def layernorm(x, g, b):
    xf = x.astype(jnp.float32)
    mu = jnp.mean(xf, axis=-1, keepdims=True)
    xc = xf - mu
    var = jnp.mean(xc * xc, axis=-1, keepdims=True)
    y = xc * lax.rsqrt(var + LN_EPS) * g.astype(jnp.float32) + b.astype(jnp.float32)
    return y.astype(x.dtype)


def causal_depthwise_conv(u, hist, w, b=None):
    full = jnp.concatenate([hist.astype(u.dtype), u], axis=1)
    y = lax.conv_general_dilated(full, w[:, None, :].astype(u.dtype), window_strides=(1,),
                                 padding='VALID', dimension_numbers=('NWC', 'WIO', 'NWC'),
                                 feature_group_count=u.shape[-1])
    if b is not None:
        y = y + b
    return y, full[:, full.shape[1] - hist.shape[1]:]


def stick_breaking(q, k, v, bias, q_pos, k_pos):
    z = jnp.einsum('nqhd,nkhd->nhqk', q, k).astype(jnp.float32) * (SB_HEAD_DIM ** -0.5)
    z = z + bias.astype(jnp.float32)[None, :, None, None]
    mask = k_pos[None, :] < q_pos[:, None]
    log_keep = jnp.where(mask, jax.nn.log_sigmoid(-z), 0.0)
    suffix = lax.cumsum(log_keep, axis=3, reverse=True) - log_keep
    a = jnp.where(mask, jnp.exp(jax.nn.log_sigmoid(z) + suffix), 0.0)
    return jnp.einsum('nhqk,nkhd->nqhd', a.astype(v.dtype), v)


def prompt_attention(q, k, v, bias):
    n, t, h, d = q.shape
    n_blocks = t // Q_BLOCK
    q_blocks = jnp.moveaxis(q.reshape(n, n_blocks, Q_BLOCK, h, d), 1, 0)
    k_pos = jnp.arange(t)

    def one_block(args):
        q_blk, blk = args
        q_pos = blk * Q_BLOCK + jnp.arange(Q_BLOCK)
        return stick_breaking(q_blk, k, v, bias, q_pos, k_pos)

    out = lax.map(one_block, (q_blocks, jnp.arange(n_blocks)))
    return jnp.moveaxis(out, 0, 1).reshape(n, t, h, d)


def sample_attention(q, k, v, bias, k_past, v_past):
    past = k_past.shape[1]
    t = q.shape[1]
    k_all = jnp.concatenate([k_past.astype(k.dtype), k], axis=1)
    v_all = jnp.concatenate([v_past.astype(v.dtype), v], axis=1)
    return stick_breaking(q, k_all, v_all, bias, past + jnp.arange(t), jnp.arange(past + t))


def hier_moe(h, w_rg, b_rg, w_re, b_re, w_gate, w_up, w_down):
    lead = h.shape[:-1]
    hf = h.reshape(-1, D_MODEL)
    g_logits = (hf @ w_rg).astype(jnp.float32) + b_rg.astype(jnp.float32)
    g_prob = jax.nn.softmax(g_logits, axis=-1)
    g_idx = jnp.argmax(g_logits, axis=-1)
    g_w = jnp.take_along_axis(g_prob, g_idx[:, None], axis=-1)
    e_logits = ((hf @ w_re).astype(jnp.float32) + b_re.astype(jnp.float32)).reshape(-1, N_GROUPS, EXPERTS_PER_GROUP)
    e_in_group = jnp.take_along_axis(e_logits, g_idx[:, None, None], axis=1)[:, 0]
    top_vals, top_idx = lax.top_k(e_in_group, TOP_K)
    top_w = jax.nn.softmax(top_vals, axis=-1) * g_w
    expert_id = g_idx[:, None] * EXPERTS_PER_GROUP + top_idx
    combine = jnp.sum(jax.nn.one_hot(expert_id, N_EXPERTS, dtype=jnp.float32) * top_w[..., None], axis=1)
    act = jax.nn.silu(jnp.einsum('nd,edf->nef', hf, w_gate)) * jnp.einsum('nd,edf->nef', hf, w_up)
    act = act * combine[:, :, None].astype(act.dtype)
    out = jnp.einsum('nef,efd->nd', act, w_down)
    return out.reshape(*lead, D_MODEL)


def trunk_layer(x, hist_a, hist_c, attend, params):
    (norm_mix, w_in, conv_a_w, w_out_a, sb_bias, w_out_b, conv_c_w, conv_c_b, ln_c_g, ln_c_b, w_out_c,
     w_o, norm_ffn, w_rg, b_rg, w_re, b_re, w_gate, w_up, w_down) = params
    n, t, _ = x.shape
    h = rmsnorm(x, norm_mix)
    proj = h @ w_in
    offsets = [int(o) for o in np.cumsum(SPLIT_SIZES)[:-1]]
    a_b, a_c, a_x, q, k, v, c_val, c_gate, gate_logits = jnp.split(proj, offsets, axis=-1)
    conv_a, new_hist_a = causal_depthwise_conv(a_c * a_x, hist_a, conv_a_w)
    y_a = (a_b * conv_a) @ w_out_a
    q = q.reshape(n, t, SB_HEADS, SB_HEAD_DIM)
    k = k.reshape(n, t, SB_HEADS, SB_HEAD_DIM)
    v = v.reshape(n, t, SB_HEADS, SB_HEAD_DIM)
    y_b = attend(q, k, v, sb_bias).reshape(n, t, SB_WIDTH) @ w_out_b
    glu = c_val * jax.nn.sigmoid(c_gate)
    conv_c, new_hist_c = causal_depthwise_conv(glu, hist_c, conv_c_w, conv_c_b)
    y_c = jax.nn.silu(layernorm(conv_c, ln_c_g, ln_c_b)) @ w_out_c
    g = jax.nn.sigmoid(gate_logits).reshape(n, t, N_BRANCH, D_MODEL)
    merged = g[:, :, 0] * y_a + g[:, :, 1] * y_b + g[:, :, 2] * y_c
    x = x + merged @ w_o
    x = x + hier_moe(rmsnorm(x, norm_ffn), w_rg, b_rg, w_re, b_re, w_gate, w_up, w_down)
    return x, k, v, new_hist_a, new_hist_c


def setup_inputs(seed: int = 0) -> dict:
    key = jax.random.key(seed)
    ks = jax.random.split(key, 32)
    f32 = jnp.float32
    n_pages = PAST_LEN // PAGE_SIZE
    n_pool = (DEC_BATCH * n_pages * 5 + 3) // 4

    def nrm(k, shape, scale):
        return jax.random.normal(k, shape, f32) * scale

    page_table = jax.random.permutation(ks[6], n_pool)[:DEC_BATCH * n_pages].reshape(DEC_BATCH, n_pages).astype(jnp.int32)
    return {
        'x_prompt': nrm(ks[0], (BATCH, SEQ, D_MODEL), 1.0),
        'x_sample': nrm(ks[1], (DEC_BATCH, DEC_SEQ, D_MODEL), 1.0),
        'cache_k': nrm(ks[2], (DEPTH, n_pool, PAGE_SIZE, SB_HEADS, SB_HEAD_DIM), 1.0),
        'cache_v': nrm(ks[3], (DEPTH, n_pool, PAGE_SIZE, SB_HEADS, SB_HEAD_DIM), 1.0),
        'state_conv_a': nrm(ks[4], (DEPTH, DEC_BATCH, A_CONV - 1, A_WIDTH), 1.0),
        'state_conv_c': nrm(ks[5], (DEPTH, DEC_BATCH, C_CONV - 1, C_WIDTH), 0.5),
        'page_table': page_table,
        'norm_mix': 1.0 + nrm(ks[7], (DEPTH, D_MODEL), 0.02),
        'w_in': nrm(ks[8], (DEPTH, D_MODEL, IN_COLS), D_MODEL ** -0.5),
        'conv_a_w': nrm(ks[9], (DEPTH, A_CONV, A_WIDTH), A_CONV ** -0.5),
        'w_out_a': nrm(ks[10], (DEPTH, A_WIDTH, D_MODEL), A_WIDTH ** -0.5),
        'sb_bias': SB_BIAS_INIT + nrm(ks[27], (DEPTH, SB_HEADS), 0.1),
        'w_out_b': nrm(ks[11], (DEPTH, SB_WIDTH, D_MODEL), SB_WIDTH ** -0.5),
        'conv_c_w': nrm(ks[12], (DEPTH, C_CONV, C_WIDTH), C_CONV ** -0.5),
        'conv_c_b': nrm(ks[13], (DEPTH, C_WIDTH), 0.02),
        'ln_c_g': 1.0 + nrm(ks[14], (DEPTH, C_WIDTH), 0.02),
        'ln_c_b': nrm(ks[15], (DEPTH, C_WIDTH), 0.02),
        'w_out_c': nrm(ks[16], (DEPTH, C_WIDTH, D_MODEL), C_WIDTH ** -0.5),
        'w_o': nrm(ks[17], (DEPTH, D_MODEL, D_MODEL), D_MODEL ** -0.5),
        'norm_ffn': 1.0 + nrm(ks[18], (DEPTH, D_MODEL), 0.02),
        'w_router_group': nrm(ks[19], (DEPTH, D_MODEL, N_GROUPS), D_MODEL ** -0.5),
        'b_router_group': nrm(ks[20], (DEPTH, N_GROUPS), 0.01),
        'w_router_expert': nrm(ks[21], (DEPTH, D_MODEL, N_EXPERTS), D_MODEL ** -0.5),
        'b_router_expert': nrm(ks[22], (DEPTH, N_EXPERTS), 0.01),
        'w_gate': nrm(ks[23], (DEPTH, N_EXPERTS, D_MODEL, D_EXPERT), D_MODEL ** -0.5),
        'w_up': nrm(ks[24], (DEPTH, N_EXPERTS, D_MODEL, D_EXPERT), D_MODEL ** -0.5),
        'w_down': nrm(ks[25], (DEPTH, N_EXPERTS, D_EXPERT, D_MODEL), D_EXPERT ** -0.5),
        'norm_final': 1.0 + nrm(ks[26], (D_MODEL,), 0.02),
    }


def reference(x_prompt, x_sample, cache_k, cache_v, state_conv_a, state_conv_c, page_table,
              norm_mix, w_in, conv_a_w, w_out_a, sb_bias, w_out_b, conv_c_w, conv_c_b, ln_c_g, ln_c_b,
              w_out_c, w_o, norm_ffn, w_router_group, b_router_group, w_router_expert,
              b_router_expert, w_gate, w_up, w_down, norm_final):
    n_seq, n_pages = page_table.shape
    past_len = n_pages * cache_k.shape[2]
    bp = x_prompt.shape[0]
    yp, ys = x_prompt, x_sample
    k_p, v_p, k_s, v_s, ca_p, ca_s, cc_p, cc_s = [], [], [], [], [], [], [], []
    for l in range(DEPTH):
        params = (norm_mix[l], w_in[l], conv_a_w[l], w_out_a[l], sb_bias[l], w_out_b[l], conv_c_w[l],
                  conv_c_b[l], ln_c_g[l], ln_c_b[l], w_out_c[l], w_o[l], norm_ffn[l], w_router_group[l],
                  b_router_group[l], w_router_expert[l], b_router_expert[l], w_gate[l], w_up[l], w_down[l])
        zeros_a = jnp.zeros((bp, A_CONV - 1, A_WIDTH), x_prompt.dtype)
        zeros_c = jnp.zeros((bp, C_CONV - 1, C_WIDTH), x_prompt.dtype)
        yp, kp, vp, hap, hcp = trunk_layer(yp, zeros_a, zeros_c, prompt_attention, params)
        k_past = cache_k[l][page_table].reshape(n_seq, past_len, SB_HEADS, SB_HEAD_DIM)
        v_past = cache_v[l][page_table].reshape(n_seq, past_len, SB_HEADS, SB_HEAD_DIM)
        attend = functools.partial(sample_attention, k_past=k_past, v_past=v_past)
        ys, ks_, vs_, has, hcs = trunk_layer(ys, state_conv_a[l], state_conv_c[l], attend, params)
        k_p.append(kp); v_p.append(vp); k_s.append(ks_); v_s.append(vs_)
        ca_p.append(hap); ca_s.append(has); cc_p.append(hcp); cc_s.append(hcs)
    return (rmsnorm(yp, norm_final), rmsnorm(ys, norm_final),
            jnp.stack(k_p), jnp.stack(v_p), jnp.stack(k_s), jnp.stack(v_s),
            jnp.stack(ca_p), jnp.stack(ca_s), jnp.stack(cc_p), jnp.stack(cc_s))
```

```python
import functools
import math

import jax
import jax.numpy as jnp
from jax import lax
from jax.experimental import pallas as pl
from jax.experimental.pallas import tpu as pltpu

BF16 = jnp.bfloat16
F32 = jnp.float32
RMS_EPS = 1e-6
LN_EPS = 1e-5
TOP_K = 2
LANES = 128
V7X_VMEM_BUDGET = 56 * 1024 * 1024
NEG = -3.0e38
NT_DIMS = (((1,), (1,)), ((), ()))


def _tile(n, target, align):
    if n <= target:
        return n
    t = (target // align) * align
    while t >= align:
        if n % t == 0:
            return t
        t -= align
    raise ValueError(f"no tile for {n} (target {target}, align {align})")


def _params(*sem):
    return pltpu.CompilerParams(dimension_semantics=sem or None, vmem_limit_bytes=V7X_VMEM_BUDGET)


def _softplus(z):
    return jnp.maximum(z, 0.0) + jnp.log(1.0 + jnp.exp(-jnp.abs(z)))


def _split_bf16(x):
    hi = x.astype(BF16)
    lo = (x - hi.astype(F32)).astype(BF16)
    return hi, lo


def _dot(a, b):
    return jnp.dot(a, b, preferred_element_type=F32)


def _dot3(a_hi, a_lo, w):
    w_hi, w_lo = _split_bf16(w)
    return _dot(a_hi, w_hi) + _dot(a_hi, w_lo) + _dot(a_lo, w_hi)


def _rms(x, g):
    ms = jnp.mean(x * x, axis=-1, keepdims=True)
    return x * lax.rsqrt(ms + RMS_EPS) * g


def _rmsnorm_kernel(x_ref, g_ref, o_ref):
    o_ref[...] = _rms(x_ref[...], g_ref[...]).astype(o_ref.dtype)


def _rmsnorm(x, g, out_dtype):
    n, d = x.shape
    tm = _tile(n, 512, 16)
    return pl.pallas_call(
        _rmsnorm_kernel,
        out_shape=jax.ShapeDtypeStruct((n, d), out_dtype),
        grid=(n // tm,),
        in_specs=[pl.BlockSpec((tm, d), lambda i: (i, 0)), pl.BlockSpec((1, d), lambda i: (0, 0))],
        out_specs=pl.BlockSpec((tm, d), lambda i: (i, 0)),
        compiler_params=_params("arbitrary"),
        name="rmsnorm",
    )(x, g.reshape(1, d))


def _proj_kernel(h_ref, w_ref, *o_refs, sigmoid_from):
    acc = _dot(h_ref[...], w_ref[...].astype(BF16))

    def store(val):
        for o_ref in o_refs:
            o_ref[...] = val.astype(o_ref.dtype)

    if sigmoid_from is None:
        store(acc)
    else:
        j = pl.program_id(1)

        @pl.when(j >= sigmoid_from)
        def _():
            store(jax.nn.sigmoid(acc))

        @pl.when(j < sigmoid_from)
        def _():
            store(acc)


def _proj(h, w_in, layer, c0, c1, tn, out_dtypes, sigmoid_from=None, name="proj"):
    n, k = h.shape
    tm = _tile(n, 2176, 16)
    cb0 = c0 // tn
    outs = pl.pallas_call(
        functools.partial(_proj_kernel, sigmoid_from=sigmoid_from),
        out_shape=tuple(jax.ShapeDtypeStruct((n, c1 - c0), dt) for dt in out_dtypes),
        grid=(n // tm, (c1 - c0) // tn),
        in_specs=[pl.BlockSpec((tm, k), lambda i, j: (i, 0)),
                  pl.BlockSpec((None, k, tn), lambda i, j: (layer, 0, cb0 + j))],
        out_specs=tuple(pl.BlockSpec((tm, tn), lambda i, j: (i, j)) for _ in out_dtypes),
        compiler_params=_params("arbitrary", "arbitrary"),
        name=name,
    )(h, w_in)
    return outs if len(out_dtypes) > 1 else outs[0]


def _tail_proj_kernel(x_ref, g_ref, w_ref, o_ref, hh_ref, hl_ref):
    @pl.when(pl.program_id(0) == 0)
    def _():
        hi, lo = _split_bf16(_rms(x_ref[...], g_ref[...]))
        hh_ref[...] = hi
        hl_ref[...] = lo

    o_ref[...] = _dot3(hh_ref[...], hl_ref[...], w_ref[...])


def _tail_proj(x_t, g, w_in, layer, c0, c1, tn, name):
    r, d = x_t.shape
    cb0 = c0 // tn
    return pl.pallas_call(
        _tail_proj_kernel,
        out_shape=jax.ShapeDtypeStruct((r, c1 - c0), F32),
        grid=((c1 - c0) // tn,),
        in_specs=[pl.BlockSpec((r, d), lambda j: (0, 0)), pl.BlockSpec((1, d), lambda j: (0, 0)),
                  pl.BlockSpec((None, d, tn), lambda j: (layer, 0, cb0 + j))],
        out_specs=pl.BlockSpec((r, tn), lambda j: (0, j)),
        scratch_shapes=[pltpu.VMEM((r, d), BF16), pltpu.VMEM((r, d), BF16)],
        compiler_params=_params("arbitrary"),
        name=name,
    )(x_t, g.reshape(1, d), w_in)


def _ln_silu(acc, g, b):
    mu = jnp.mean(acc, axis=-1, keepdims=True)
    xc = acc - mu
    var = jnp.mean(xc * xc, axis=-1, keepdims=True)
    y = xc * lax.rsqrt(var + LN_EPS) * g + b
    return y * jax.nn.sigmoid(y)


def _mix_prompt_kernel(ab_ref, ac_ref, ax_ref, cv_ref, cg_ref, caw_ref, ccw_ref, ccb_ref, lng_ref, lnb_ref,
                       af_ref, cf_ref, ha_ref, hc_ref, exa_ref, exc_ref, *, tt, ka, kc, pa, pc, gate_is_logit):
    ti = pl.program_id(1)

    @pl.when(ti == 0)
    def _():
        exa_ref[0:pa, :] = jnp.zeros((pa, exa_ref.shape[1]), F32)
        exc_ref[0:pc, :] = jnp.zeros((pc, exc_ref.shape[1]), F32)

    u = ac_ref[...].astype(F32) * ax_ref[...].astype(F32)
    exa_ref[pa:pa + tt, :] = u
    conv = caw_ref[ka - 1:ka, :] * u
    for j in range(ka - 1):
        conv = conv + caw_ref[j:j + 1, :] * exa_ref[pa - (ka - 1) + j:pa - (ka - 1) + j + tt, :]
    af_ref[...] = (ab_ref[...].astype(F32) * conv).astype(af_ref.dtype)

    cg = cg_ref[...].astype(F32)
    glu = cv_ref[...].astype(F32) * (jax.nn.sigmoid(cg) if gate_is_logit else cg)
    exc_ref[pc:pc + tt, :] = glu
    acc = ccb_ref[...] + ccw_ref[kc - 1:kc, :] * glu
    for j in range(kc - 1):
        acc = acc + ccw_ref[j:j + 1, :] * exc_ref[pc - (kc - 1) + j:pc - (kc - 1) + j + tt, :]
    cf_ref[...] = _ln_silu(acc, lng_ref[...], lnb_ref[...]).astype(cf_ref.dtype)

    @pl.when(ti == pl.num_programs(1) - 1)
    def _():
        ha_ref[...] = exa_ref[pa + tt - (ka - 1):pa + tt, :]
        hc_ref[...] = exc_ref[pc + tt - (kc - 1):pc + tt, :]

    exa_ref[0:pa, :] = exa_ref[tt:tt + pa, :]
    exc_ref[0:pc, :] = exc_ref[tt:tt + pc, :]


def _mix_prompt(pa_arr, pc_arr, bp, seq, aw, cw, caw, ccw, ccb, lng, lnb, out_dtype=None, gate_is_logit=False,
                name="mix_prompt"):
    out_dtype = out_dtype or BF16
    ka, kc = caw.shape[0], ccw.shape[0]
    tt = _tile(seq, 512, 32)
    nt = seq // tt
    pa, pc = 8, 32
    assert ka - 1 <= pa and kc - 1 <= pc and tt >= pc
    row = lambda c: (lambda b, t: (b * nt + t, c))
    const = lambda b, t: (0, 0)
    return pl.pallas_call(
        functools.partial(_mix_prompt_kernel, tt=tt, ka=ka, kc=kc, pa=pa, pc=pc, gate_is_logit=gate_is_logit),
        out_shape=(jax.ShapeDtypeStruct((bp * seq, aw), out_dtype), jax.ShapeDtypeStruct((bp * seq, cw), out_dtype),
                   jax.ShapeDtypeStruct((bp, ka - 1, aw), F32), jax.ShapeDtypeStruct((bp, kc - 1, cw), F32)),
        grid=(bp, nt),
        in_specs=[pl.BlockSpec((tt, aw), row(0)), pl.BlockSpec((tt, aw), row(1)), pl.BlockSpec((tt, aw), row(2)),
                  pl.BlockSpec((tt, cw), row(0)), pl.BlockSpec((tt, cw), row(1)),
                  pl.BlockSpec((ka, aw), const), pl.BlockSpec((kc, cw), const), pl.BlockSpec((1, cw), const),
                  pl.BlockSpec((1, cw), const), pl.BlockSpec((1, cw), const)],
        out_specs=(pl.BlockSpec((tt, aw), row(0)), pl.BlockSpec((tt, cw), row(0)),
                   pl.BlockSpec((None, ka - 1, aw), lambda b, t: (b, 0, 0)),
                   pl.BlockSpec((None, kc - 1, cw), lambda b, t: (b, 0, 0))),
        scratch_shapes=[pltpu.VMEM((pa + tt, aw), F32), pltpu.VMEM((pc + tt, cw), F32)],
        compiler_params=_params("arbitrary", "arbitrary"),
        name=name,
    )(pa_arr, pa_arr, pa_arr, pc_arr, pc_arr, caw, ccw, ccb.reshape(1, cw), lng.reshape(1, cw), lnb.reshape(1, cw))


def _mix_sample_kernel(pa_ref, pc_ref, ha_ref, hc_ref, caw_ref, ccw_ref, ccb_ref, lng_ref, lnb_ref,
                       af_ref, cf_ref, nha_ref, nhc_ref, *, aw, cw):
    ds = pa_ref.shape[0]
    ka, kc = caw_ref.shape[0], ccw_ref.shape[0]
    u = [pa_ref[t, :, aw:2 * aw].astype(F32) * pa_ref[t, :, 2 * aw:3 * aw].astype(F32) for t in range(ds)]
    glu = [pc_ref[t, :, 0:cw].astype(F32) * pc_ref[t, :, cw:2 * cw].astype(F32) for t in range(ds)]
    full_a = lambda i: ha_ref[i] if i < ka - 1 else u[i - (ka - 1)]
    full_c = lambda i: hc_ref[i] if i < kc - 1 else glu[i - (kc - 1)]
    for t in range(ds):
        conv = caw_ref[0:1, :] * full_a(t)
        for j in range(1, ka):
            conv = conv + caw_ref[j:j + 1, :] * full_a(t + j)
        af_ref[t] = (pa_ref[t, :, 0:aw].astype(F32) * conv).astype(af_ref.dtype)
        acc = ccb_ref[...] + ccw_ref[0:1, :] * full_c(t)
        for j in range(1, kc):
            acc = acc + ccw_ref[j:j + 1, :] * full_c(t + j)
        cf_ref[t] = _ln_silu(acc, lng_ref[...], lnb_ref[...]).astype(cf_ref.dtype)
    for i in range(ka - 1):
        nha_ref[i] = full_a(i + ds)
    for i in range(kc - 1):
        nhc_ref[i] = full_c(i + ds)


def _mix_sample(pa_t, pc_t, ha_t, hc_t, aw, cw, caw, ccw, ccb, lng, lnb):
    ds, db, _ = pa_t.shape
    ka, kc = caw.shape[0], ccw.shape[0]
    return pl.pallas_call(
        functools.partial(_mix_sample_kernel, aw=aw, cw=cw),
        out_shape=(jax.ShapeDtypeStruct((ds, db, aw), BF16), jax.ShapeDtypeStruct((ds, db, cw), BF16),
                   jax.ShapeDtypeStruct((ka - 1, db, aw), F32), jax.ShapeDtypeStruct((kc - 1, db, cw), F32)),
        compiler_params=_params(),
        name="mix_sample",
    )(pa_t, pc_t, ha_t, hc_t, caw, ccw, ccb.reshape(1, cw), lng.reshape(1, cw), lnb.reshape(1, cw))


def _suffix_matrix(bk):
    j = lax.broadcasted_iota(jnp.int32, (bk, bk), 0)
    s = lax.broadcasted_iota(jnp.int32, (bk, bk), 1)
    return (j >= s).astype(BF16)


def _sb_block(z, mask, tmat, carry):
    bk = z.shape[1]
    sp = _softplus(z)
    if mask is not None:
        sp = jnp.where(mask, sp, 0.0)
    hi, lo = _split_bf16(sp)
    suf = _dot(hi, tmat) + _dot(lo, tmat)
    c = suf + jnp.concatenate([carry] * (bk // LANES), axis=1)
    a = jnp.exp(z - c)
    if mask is not None:
        a = jnp.where(mask, a, 0.0)
    return a, carry + jnp.broadcast_to(suf[:, 0:1], carry.shape)


def _attn_prompt_kernel(bias_ref, q_ref, k_ref, v_ref, tmat_ref, o_ref, acc_ref, carry_ref,
                        *, tq, hd, hg, qi_base, scale, precise):
    g = pl.program_id(1)
    qi = qi_base + pl.program_id(2)
    tmat = tmat_ref[...]

    qs = []
    for h in range(hg):
        qh = q_ref[:, h * hd:(h + 1) * hd]
        qs.append(_split_bf16(qh) if precise else (qh.astype(BF16),))

    def logits(h, start):
        kblk = k_ref[pl.ds(start, tq), h * hd:(h + 1) * hd]
        s = sum(lax.dot_general(part, kblk, NT_DIMS, preferred_element_type=F32) for part in qs[h])
        return s * scale + bias_ref[g * hg + h]

    d0 = pl.multiple_of(qi * tq, tq)
    r = lax.broadcasted_iota(jnp.int32, (tq, tq), 0)
    c = lax.broadcasted_iota(jnp.int32, (tq, tq), 1)
    for h in range(hg):
        a, carry = _sb_block(logits(h, d0), c < r, tmat, jnp.zeros((tq, LANES), F32))
        acc_ref[:, h * hd:(h + 1) * hd] = _dot(a.astype(BF16), v_ref[pl.ds(d0, tq), h * hd:(h + 1) * hd])
        carry_ref[h] = carry

    def body(it, _):
        s0 = pl.multiple_of((qi - 1 - it) * tq, tq)
        for h in range(hg):
            a, carry = _sb_block(logits(h, s0), None, tmat, carry_ref[h])
            acc_ref[:, h * hd:(h + 1) * hd] += _dot(a.astype(BF16), v_ref[pl.ds(s0, tq), h * hd:(h + 1) * hd])
            carry_ref[h] = carry
        return 0

    lax.fori_loop(0, qi, body, 0)
    o_ref[...] = acc_ref[...].astype(o_ref.dtype)


def _attn_prompt(q, kvb, bias, bp, seq, nh, hd, tq, nq, qi_base, out_dtype, precise, name):
    hg = 4 if nh % 4 == 0 else 2
    ng = nh // hg
    return pl.pallas_call(
        functools.partial(_attn_prompt_kernel, tq=tq, hd=hd, hg=hg, qi_base=qi_base, scale=hd ** -0.5,
                          precise=precise),
        out_shape=jax.ShapeDtypeStruct((bp * nq * tq, nh * hd), out_dtype),
        grid=(bp, ng, nq),
        in_specs=[pl.BlockSpec(memory_space=pltpu.SMEM),
                  pl.BlockSpec((tq, hg * hd), lambda b, g, i: (b * nq + i, g)),
                  pl.BlockSpec((seq, hg * hd), lambda b, g, i: (b, g)),
                  pl.BlockSpec((seq, hg * hd), lambda b, g, i: (b, ng + g)),
                  pl.BlockSpec((tq, tq), lambda b, g, i: (0, 0))],
        out_specs=pl.BlockSpec((tq, hg * hd), lambda b, g, i: (b * nq + i, g)),
        scratch_shapes=[pltpu.VMEM((tq, hg * hd), F32), pltpu.VMEM((hg, tq, LANES), F32)],
        compiler_params=_params("arbitrary", "arbitrary", "arbitrary"),
        name=name,
    )(bias, q, kvb, kvb, _suffix_matrix(tq))


def _attn_sample_kernel(pt_ref, bias_ref, q_ref, kn_ref, vn_ref, tmat_ref, ck_hbm, cv_hbm, o_ref,
                        knf_ref, vnf_ref, kbuf, vbuf, sem, *, layer, npg, ds, hd, nh, page, scale):
    n = pl.program_id(0)
    rr, ww = 2 * ds, 2 * hd
    p2 = nh // 2
    rows = p2 * rr

    def page_copies(seq_idx, slot):
        for pg in range(npg):
            src = pt_ref[seq_idx * npg + pg]
            yield pltpu.make_async_copy(ck_hbm.at[layer, src], kbuf.at[slot, pg], sem.at[slot])
            yield pltpu.make_async_copy(cv_hbm.at[layer, src], vbuf.at[slot, pg], sem.at[slot])

    @pl.when(n == 0)
    def _():
        for cp in page_copies(0, 0):
            cp.start()

    @pl.when(n + 1 < pl.num_programs(0))
    def _():
        for cp in page_copies(n + 1, (n + 1) % 2):
            cp.start()

    slot = n % 2
    for cp in page_copies(n, slot):
        cp.wait()

    knf_ref[...] = jnp.zeros(knf_ref.shape, F32)
    vnf_ref[...] = jnp.zeros(vnf_ref.shape, F32)
    knf_ref[0:ds, :] = kn_ref[...]
    vnf_ref[0:ds, :] = vn_ref[...]

    tmat = tmat_ref[...]
    row = lax.broadcasted_iota(jnp.int32, (rows, page), 0)
    col = lax.broadcasted_iota(jnp.int32, (rows, page), 1)
    head = (row // rr) * 2 + (row % rr) // ds
    bias = jnp.zeros((rows, page), F32)
    for hh in range(nh):
        bias = jnp.where(head == hh, bias_ref[hh], bias)
    new_mask = col < (row % ds)

    lhs = [q_ref[p * rr:(p + 1) * rr, :].astype(BF16) for p in range(p2)]

    def new_tile(ref, p):
        return ref[:, p * ww:(p + 1) * ww].astype(BF16)

    def page_tile(buf, pg, p):
        view = buf.at[slot, pg]
        return jnp.concatenate([view[pl.ds(2 * p + e, page, stride=nh), :] for e in range(2)], axis=1).astype(BF16)

    def block(k_tile, v_tile, mask, carry, acc):
        s = jnp.concatenate(
            [lax.dot_general(lhs[p], k_tile(p), NT_DIMS, preferred_element_type=F32) for p in range(p2)], axis=0)
        a, carry = _sb_block(s * scale + bias, mask, tmat, carry)
        acc = [acc[p] + _dot(a[p * rr:(p + 1) * rr, :].astype(BF16), v_tile(p)) for p in range(p2)]
        return carry, acc

    carry = jnp.zeros((rows, LANES), F32)
    acc = [jnp.zeros((rr, ww), F32) for _ in range(p2)]
    carry, acc = block(functools.partial(new_tile, knf_ref), functools.partial(new_tile, vnf_ref), new_mask, carry, acc)
    for pg in range(npg - 1, -1, -1):
        carry, acc = block(functools.partial(page_tile, kbuf, pg), functools.partial(page_tile, vbuf, pg), None,
                           carry, acc)
    for p in range(p2):
        o_ref[p * rr:(p + 1) * rr, :] = acc[p]


def _attn_sample(q_bd, kn, vn, cache_k, cache_v, layer, page_table, bias, ds, nh, hd):
    db = q_bd.shape[0]
    npg = page_table.shape[1]
    depth, npool, page = cache_k.shape[:3]
    hw = nh * hd
    assert page == LANES and hd == LANES and (2 * ds) % 8 == 0 and nh % 2 == 0
    ck = cache_k.reshape(depth, npool, page * nh, hd)
    cv = cache_v.reshape(depth, npool, page * nh, hd)
    rows = (nh // 2) * 2 * ds
    seq_blk = lambda shape: pl.BlockSpec((None,) + shape, lambda n, pt: (n, 0, 0))
    grid_spec = pltpu.PrefetchScalarGridSpec(
        num_scalar_prefetch=1,
        grid=(db,),
        in_specs=[pl.BlockSpec(memory_space=pltpu.SMEM), seq_blk((rows, 2 * hd)), seq_blk((ds, hw)), seq_blk((ds, hw)),
                  pl.BlockSpec((page, page), lambda n, pt: (0, 0)),
                  pl.BlockSpec(memory_space=pl.ANY), pl.BlockSpec(memory_space=pl.ANY)],
        out_specs=seq_blk((rows, 2 * hd)),
        scratch_shapes=[pltpu.VMEM((page, hw), F32), pltpu.VMEM((page, hw), F32),
                        pltpu.VMEM((2, npg, page * nh, hd), F32), pltpu.VMEM((2, npg, page * nh, hd), F32),
                        pltpu.SemaphoreType.DMA((2,))],
    )
    return pl.pallas_call(
        functools.partial(_attn_sample_kernel, layer=layer, npg=npg, ds=ds, hd=hd, nh=nh, page=page,
                          scale=hd ** -0.5),
        out_shape=jax.ShapeDtypeStruct((db, rows, 2 * hd), F32),
        grid_spec=grid_spec,
        compiler_params=_params("arbitrary"),
        name="attn_sample",
    )(page_table.reshape(-1), bias, q_bd, kn, vn, _suffix_matrix(page), ck, cv)


def _merge_kernel(afp_ref, atp_ref, cfp_ref, afs_ref, ats_ref, cfs_ref, g0_ref, g1_ref, g2_ref,
                  wa_ref, wb_ref, wc_ref, o_ref, *, n_prompt_tiles):
    def run(af_ref, at_ref, cf_ref):
        ya = _dot(af_ref[...], wa_ref[...])
        yb = _dot(at_ref[...], wb_ref[...])
        yc = _dot(cf_ref[...], wc_ref[...])
        o_ref[...] = (g0_ref[...].astype(F32) * ya + g1_ref[...].astype(F32) * yb
                      + g2_ref[...].astype(F32) * yc).astype(o_ref.dtype)

    i = pl.program_id(0)

    @pl.when(i < n_prompt_tiles)
    def _():
        run(afp_ref, atp_ref, cfp_ref)

    @pl.when(i >= n_prompt_tiles)
    def _():
        run(afs_ref, ats_ref, cfs_ref)


def _merge(feats_p, feats_s, gates, wa, wb, wc, tm):
    n_p, n_s = feats_p[0].shape[0], feats_s[0].shape[0]
    d = wa.shape[1]
    npt, nst = n_p // tm, n_s // tm
    prow = lambda a: pl.BlockSpec((tm, a.shape[1]), lambda i: (jnp.minimum(i, npt - 1), 0))
    srow = lambda a: pl.BlockSpec((tm, a.shape[1]), lambda i: (jnp.maximum(i - npt, 0), 0))
    gate = lambda k: pl.BlockSpec((tm, d), lambda i: (i, k))
    resident = lambda w: pl.BlockSpec(w.shape, lambda i: (0, 0), pipeline_mode=pl.Buffered(1))
    return pl.pallas_call(
        functools.partial(_merge_kernel, n_prompt_tiles=npt),
        out_shape=jax.ShapeDtypeStruct((n_p + n_s, d), BF16),
        grid=(npt + nst,),
        in_specs=[prow(a) for a in feats_p] + [srow(a) for a in feats_s] + [gate(0), gate(1), gate(2),
                  resident(wa), resident(wb), resident(wc)],
        out_specs=pl.BlockSpec((tm, d), lambda i: (i, 0)),
        compiler_params=_params("arbitrary"),
        name="merge",
    )(*feats_p, *feats_s, gates, gates, gates, wa, wb, wc)


def _router_logits(h, wr, br):
    hh, hl = _split_bf16(h)
    return _dot3(hh, hl, wr) + br


def _outproj_kernel(m_ref, x_ref, wo_ref, g_ref, wr_ref, br_ref, xo_ref, h_ref, lg_ref):
    xn = x_ref[...] + _dot(m_ref[...], wo_ref[...])
    xo_ref[...] = xn
    h = _rms(xn, g_ref[...])
    h_ref[...] = h
    lg_ref[...] = _router_logits(h, wr_ref[...], br_ref[...])


def _outproj(merged, x, wo, g, wr, br, tm):
    n, d = x.shape
    rowb = lambda w: pl.BlockSpec((tm, w), lambda i: (i, 0))
    const = lambda shape: pl.BlockSpec(shape, lambda i: (0, 0))
    return pl.pallas_call(
        _outproj_kernel,
        out_shape=(jax.ShapeDtypeStruct((n, d), F32), jax.ShapeDtypeStruct((n, d), F32),
                   jax.ShapeDtypeStruct((n, LANES), F32)),
        grid=(n // tm,),
        in_specs=[rowb(d), rowb(d), pl.BlockSpec((d, d), lambda i: (0, 0), pipeline_mode=pl.Buffered(1)),
                  const((1, d)), const((d, LANES)), const((1, LANES))],
        out_specs=(rowb(d), rowb(d), rowb(LANES)),
        compiler_params=_params("arbitrary"),
        name="outproj",
    )(merged, x, wo, g.reshape(1, d), wr, br)


def _tail_merge_kernel(af_ref, at_ref, cf_ref, g0_ref, g1_ref, g2_ref, wa_ref, wb_ref, wc_ref, o_ref):
    y = [_dot3(*_split_bf16(f_ref[...]), w_ref[...])
         for f_ref, w_ref in ((af_ref, wa_ref), (at_ref, wb_ref), (cf_ref, wc_ref))]
    o_ref[...] = (jax.nn.sigmoid(g0_ref[...]) * y[0] + jax.nn.sigmoid(g1_ref[...]) * y[1]
                  + jax.nn.sigmoid(g2_ref[...]) * y[2])


def _tail_merge(af, at, cf, gate_logits, wa, wb, wc, layer, tn):
    r, d = af.shape[0], wa.shape[-1]
    full = lambda a: pl.BlockSpec(a.shape, lambda j: (0, 0))
    gate = lambda k: pl.BlockSpec((r, tn), lambda j: (0, k * (d // tn) + j))
    wcol = lambda w: pl.BlockSpec((None, w.shape[1], tn), lambda j: (layer, 0, j))
    return pl.pallas_call(
        _tail_merge_kernel,
        out_shape=jax.ShapeDtypeStruct((r, d), F32),
        grid=(d // tn,),
        in_specs=[full(af), full(at), full(cf), gate(0), gate(1), gate(2), wcol(wa), wcol(wb), wcol(wc)],
        out_specs=pl.BlockSpec((r, tn), lambda j: (0, j)),
        compiler_params=_params("arbitrary"),
        name="tail_merge",
    )(af, at, cf, gate_logits, gate_logits, gate_logits, wa, wb, wc)


def _tail_router_kernel(m_ref, x_ref, wo_ref, g_ref, wr_ref, br_ref, lg_ref, mh_ref, ml_ref, xn_ref, *, tn):
    j = pl.program_id(0)

    @pl.when(j == 0)
    def _():
        hi, lo = _split_bf16(m_ref[...])
        mh_ref[...] = hi
        ml_ref[...] = lo

    col = pl.multiple_of(j * tn, tn)
    xn_ref[:, pl.ds(col, tn)] = x_ref[...] + _dot3(mh_ref[...], ml_ref[...], wo_ref[...])

    @pl.when(j == pl.num_programs(0) - 1)
    def _():
        lg_ref[...] = _router_logits(_rms(xn_ref[...], g_ref[...]), wr_ref[...], br_ref[...])


def _tail_router(merged, x_t, wo, layer, g, wr, br, tn):
    r, d = x_t.shape
    const = lambda shape: pl.BlockSpec(shape, lambda j: (0, 0))
    return pl.pallas_call(
        functools.partial(_tail_router_kernel, tn=tn),
        out_shape=jax.ShapeDtypeStruct((r, LANES), F32),
        grid=(d // tn,),
        in_specs=[const((r, d)), pl.BlockSpec((r, tn), lambda j: (0, j)),
                  pl.BlockSpec((None, d, tn), lambda j: (layer, 0, j)), const((1, d)), const((d, LANES)),
                  const((1, LANES))],
        out_specs=const((r, LANES)),
        scratch_shapes=[pltpu.VMEM((r, d), BF16), pltpu.VMEM((r, d), BF16), pltpu.VMEM((r, d), F32)],
        compiler_params=_params("arbitrary"),
        name="tail_router",
    )(merged, x_t, wo, g.reshape(1, d), wr, br)


def _route_kernel(lg_ref, tri_ref, meta_ref, cnt_ref, run_ref, *, ng, ne):
    i = pl.program_id(0)

    @pl.when(i == 0)
    def _():
        run_ref[...] = jnp.zeros(run_ref.shape, F32)

    x = lg_ref[...]
    epg = ne // ng
    lane = lax.broadcasted_iota(jnp.int32, x.shape, 1).astype(F32)
    rmax = lambda v: jnp.max(v, axis=1, keepdims=True)
    rmin = lambda v: jnp.min(v, axis=1, keepdims=True)
    rsum = lambda v: jnp.sum(v, axis=1, keepdims=True)

    is_g = lane < ng
    gl = jnp.where(is_g, x, NEG)
    gmax = rmax(gl)
    gidx = rmin(jnp.where(is_g & (gl == gmax), lane, float(LANES)))
    gw = 1.0 / rsum(jnp.where(is_g, jnp.exp(jnp.minimum(x - gmax, 0.0)), 0.0))

    lo = ng + gidx * epg
    in_grp = (lane >= lo) & (lane < lo + epg)
    el = jnp.where(in_grp, x, NEG)
    v1 = rmax(el)
    i1 = rmin(jnp.where(in_grp & (el == v1), lane, float(LANES)))
    rest = in_grp & (lane != i1)
    el2 = jnp.where(rest, x, NEG)
    v2 = rmax(el2)
    i2 = rmin(jnp.where(rest & (el2 == v2), lane, float(LANES)))
    t = jnp.exp(v2 - v1)
    w1 = gw / (1.0 + t)
    w2 = gw * t / (1.0 + t)

    sel1, sel2 = lane == i1, lane == i2
    onehot = (sel1 | sel2).astype(BF16)
    rank = _dot(tri_ref[...], onehot) + run_ref[...]
    r1 = rsum(jnp.where(sel1, rank, 0.0))
    r2 = rsum(jnp.where(sel2, rank, 0.0))
    run_ref[...] += jnp.sum(onehot.astype(F32), axis=0, keepdims=True)
    cnt_ref[...] = run_ref[...]

    vals = [i1 - ng, i2 - ng, w1, w2, r1, r2]
    meta = jnp.zeros(x.shape, F32)
    for k, v in enumerate(vals):
        meta = jnp.where(lane == k, v, meta)
    meta_ref[...] = meta


def _route(logits, ng, ne, tr):
    n = logits.shape[0]
    r = lax.broadcasted_iota(jnp.int32, (tr, tr), 0)
    c = lax.broadcasted_iota(jnp.int32, (tr, tr), 1)
    tri = (c < r).astype(BF16)
    return pl.pallas_call(
        functools.partial(_route_kernel, ng=ng, ne=ne),
        out_shape=(jax.ShapeDtypeStruct((n, LANES), F32), jax.ShapeDtypeStruct((1, LANES), F32)),
        grid=(n // tr,),
        in_specs=[pl.BlockSpec((tr, LANES), lambda i: (i, 0)), pl.BlockSpec((tr, tr), lambda i: (0, 0))],
        out_specs=(pl.BlockSpec((tr, LANES), lambda i: (i, 0)), pl.BlockSpec((1, LANES), lambda i: (0, 0))),
        scratch_shapes=[pltpu.VMEM((1, LANES), F32)],
        compiler_params=_params("arbitrary"),
        name="route",
    )(logits, tri)


def _dispatch_kernel(p1_ref, p2_ref, h_hbm, xs_hbm, sem, *, n, chunk):
    nchunks = n // chunk

    def wait_chunk(slot):
        for _ in range(TOP_K):
            pltpu.make_async_copy(h_hbm.at[pl.ds(0, chunk)], xs_hbm.at[pl.ds(0, chunk)], sem.at[slot]).wait()

    def body(c, carry):
        slot = c % 2

        def row(r, carry2):
            t = c * chunk + r
            pltpu.make_async_copy(h_hbm.at[pl.ds(t, 1)], xs_hbm.at[pl.ds(p1_ref[t], 1)], sem.at[slot]).start()
            pltpu.make_async_copy(h_hbm.at[pl.ds(t, 1)], xs_hbm.at[pl.ds(p2_ref[t], 1)], sem.at[slot]).start()
            return carry2

        lax.fori_loop(0, chunk, row, 0, unroll=8)

        @pl.when(c > 0)
        def _():
            wait_chunk(1 - slot)

        return carry

    lax.fori_loop(0, nchunks, body, 0)
    wait_chunk((nchunks - 1) % 2)


def _dispatch(h2, pos1, pos2):
    n, d = h2.shape
    chunk = _tile(n, 512, 8)
    return pl.pallas_call(
        functools.partial(_dispatch_kernel, n=n, chunk=chunk),
        out_shape=jax.ShapeDtypeStruct((TOP_K * n, d), h2.dtype),
        in_specs=[pl.BlockSpec(memory_space=pltpu.SMEM), pl.BlockSpec(memory_space=pltpu.SMEM),
                  pl.BlockSpec(memory_space=pl.ANY)],
        out_specs=pl.BlockSpec(memory_space=pl.ANY),
        scratch_shapes=[pltpu.SemaphoreType.DMA((2,))],
        compiler_params=_params(),
        name="dispatch",
    )(pos1, pos2, h2)


def _expert_kernel(it_ref, ie_ref, lo_ref, hi_ref, x_ref, wg_ref, wu_ref, wd_ref, y_ref, wgb, wub, wdb, cur_ref):
    g = pl.program_id(0)
    first_of_tile = (g == 0) | (it_ref[g] != it_ref[jnp.maximum(g - 1, 0)])
    lo, hi = lo_ref[g], hi_ref[g]

    @pl.when(g == 0)
    def _():
        cur_ref[0] = -1

    @pl.when(hi > lo)
    def _():
        @pl.when(cur_ref[0] != ie_ref[g])
        def _():
            wgb[...] = wg_ref[...].astype(BF16)
            wub[...] = wu_ref[...].astype(BF16)
            wdb[...] = wd_ref[...].astype(BF16)
            cur_ref[0] = ie_ref[g]

        x = x_ref[...].astype(BF16)
        gate = _dot(x, wgb[...])
        act = gate * jax.nn.sigmoid(gate) * _dot(x, wub[...])
        y = _dot(act.astype(BF16), wdb[...])
        row = lax.broadcasted_iota(jnp.int32, y.shape, 0)
        y = jnp.where((row >= lo) & (row < hi), y, 0.0)

        @pl.when(first_of_tile)
        def _():
            y_ref[...] = y

        @pl.when(jnp.logical_not(first_of_tile))
        def _():
            y_ref[...] += y

    @pl.when((hi <= lo) & first_of_tile)
    def _():
        y_ref[...] = jnp.zeros(y_ref.shape, F32)


def _experts(x_sorted, item_tile, item_expert, item_lo, item_hi, w_gate, w_up, w_down, layer, tm):
    rows, d = x_sorted.shape
    f = w_gate.shape[-1]
    wspec = lambda a, b: pl.BlockSpec((None, None, a, b), lambda g, it, ie, lo, hi: (layer, ie[g], 0, 0))
    grid_spec = pltpu.PrefetchScalarGridSpec(
        num_scalar_prefetch=4,
        grid=(item_tile.shape[0],),
        in_specs=[pl.BlockSpec((tm, d), lambda g, it, ie, lo, hi: (it[g], 0)), wspec(d, f), wspec(d, f), wspec(f, d)],
        out_specs=pl.BlockSpec((tm, d), lambda g, it, ie, lo, hi: (it[g], 0)),
        scratch_shapes=[pltpu.VMEM((d, f), BF16), pltpu.VMEM((d, f), BF16), pltpu.VMEM((f, d), BF16),
                        pltpu.SMEM((1,), jnp.int32)],
    )
    return pl.pallas_call(
        _expert_kernel,
        out_shape=jax.ShapeDtypeStruct((rows, d), F32),
        grid_spec=grid_spec,
        compiler_params=_params("arbitrary"),
        name="experts",
    )(item_tile, item_expert, item_lo, item_hi, x_sorted, w_gate, w_up, w_down)


def _combine_kernel(p1_ref, p2_ref, x_ref, meta_ref, y_hbm, g_ref, xo_ref, ho_ref, ybuf, sem, *, tc):
    i = pl.program_id(0)
    nsteps = pl.num_programs(0)

    def start_gather(tile, slot):
        def body(r, carry):
            t = tile * tc + r
            pltpu.make_async_copy(y_hbm.at[pl.ds(p1_ref[t], 1)], ybuf.at[slot, 0, pl.ds(r, 1)], sem.at[slot]).start()
            pltpu.make_async_copy(y_hbm.at[pl.ds(p2_ref[t], 1)], ybuf.at[slot, 1, pl.ds(r, 1)], sem.at[slot]).start()
            return carry
        lax.fori_loop(0, tc, body, 0, unroll=8)

    @pl.when(i == 0)
    def _():
        start_gather(0, 0)

    @pl.when(i + 1 < nsteps)
    def _():
        start_gather(i + 1, (i + 1) % 2)

    slot = i % 2
    for k in range(TOP_K):
        pltpu.make_async_copy(y_hbm.at[pl.ds(0, tc)], ybuf.at[slot, k], sem.at[slot]).wait()
    xo = x_ref[...] + (meta_ref[:, 2:3] * ybuf[slot, 0] + meta_ref[:, 3:4] * ybuf[slot, 1])
    xo_ref[...] = xo
    ho_ref[...] = _rms(xo, g_ref[...]).astype(ho_ref.dtype)


def _combine(x, meta, y_sorted, pos1, pos2, g_next, h_dtype):
    n, d = x.shape
    tc = _tile(n, 256, 16)
    grid_spec = pltpu.PrefetchScalarGridSpec(
        num_scalar_prefetch=2,
        grid=(n // tc,),
        in_specs=[pl.BlockSpec((tc, d), lambda i, a, b: (i, 0)), pl.BlockSpec((tc, LANES), lambda i, a, b: (i, 0)),
                  pl.BlockSpec(memory_space=pl.ANY), pl.BlockSpec((1, d), lambda i, a, b: (0, 0))],
        out_specs=(pl.BlockSpec((tc, d), lambda i, a, b: (i, 0)), pl.BlockSpec((tc, d), lambda i, a, b: (i, 0))),
        scratch_shapes=[pltpu.VMEM((2, TOP_K, tc, d), F32), pltpu.SemaphoreType.DMA((2,))],
    )
    return pl.pallas_call(
        functools.partial(_combine_kernel, tc=tc),
        out_shape=(jax.ShapeDtypeStruct((n, d), F32), jax.ShapeDtypeStruct((n, d), h_dtype)),
        grid_spec=grid_spec,
        compiler_params=_params("arbitrary"),
        name="combine",
    )(pos1, pos2, x, meta, y_sorted, g_next.reshape(1, d))


def kernel(x_prompt, x_sample, cache_k, cache_v, state_conv_a, state_conv_c, page_table, norm_mix, w_in, conv_a_w,
           w_out_a, sb_bias, w_out_b, conv_c_w, conv_c_b, ln_c_g, ln_c_b, w_out_c, w_o, norm_ffn, w_router_group,
           b_router_group, w_router_expert, b_router_expert, w_gate, w_up, w_down, norm_final):
    bp, seq, d = x_prompt.shape
    db, ds, _ = x_sample.shape
    depth = w_in.shape[0]
    nh, hd = cache_k.shape[3], cache_k.shape[4]
    hw = nh * hd
    aw, cw = state_conv_a.shape[-1], state_conv_c.shape[-1]
    ng, ne = w_router_group.shape[-1], w_router_expert.shape[-1]
    kc = conv_c_w.shape[1]
    n_p, n_s = bp * seq, db * ds
    n = n_p + n_s
    c_q, c_k, c_c, c_g, c_end = 3 * aw, 3 * aw + hw, 3 * aw + 3 * hw, 3 * aw + 3 * hw + 2 * cw, w_in.shape[-1]
    tn = 512 if all(c % 512 == 0 for c in (aw, hw, cw, d)) else LANES
    tmx = _tile(math.gcd(n_p, n_s), 512, 16)
    tme = 512 if (TOP_K * n) % 512 == 0 else 64
    assert (TOP_K * n) % tme == 0
    n_tiles = TOP_K * n // tme
    tq = _tile(seq, 256, LANES)
    nq = seq // tq
    assert tq >= 2 * kc

    x = jnp.concatenate([x_prompt.reshape(n_p, d), x_sample.reshape(n_s, d)], axis=0)
    h = _rmsnorm(x, norm_mix[0], BF16)
    outs = {k: [] for k in ("kp", "vp", "ks", "vs", "ap", "as", "cp", "cs")}

    for l in range(depth):
        pa = _proj(h, w_in, l, 0, c_q, tn, (BF16,), name="proj_a")
        q = _proj(h, w_in, l, c_q, c_k, tn, (BF16,), name="proj_q")
        kv, kvb = _proj(h, w_in, l, c_k, c_c, tn, (F32, BF16), name="proj_kv")
        pc = _proj(h, w_in, l, c_c, c_g, tn, (BF16,), sigmoid_from=cw // tn, name="proj_c")
        gates = _proj(h, w_in, l, c_g, c_end, tn, (BF16,), sigmoid_from=0, name="proj_gates")

        conv_w = (conv_a_w[l], conv_c_w[l], conv_c_b[l], ln_c_g[l], ln_c_b[l])
        af_p, cf_p, ha_p, hc_p = _mix_prompt(pa, pc, bp, seq, aw, cw, *conv_w)
        pa_t = pa[n_p:].reshape(db, ds, 3 * aw).transpose(1, 0, 2)
        pc_t = pc[n_p:].reshape(db, ds, 2 * cw).transpose(1, 0, 2)
        af_s, cf_s, ha_s, hc_s = _mix_sample(pa_t, pc_t, state_conv_a[l].transpose(1, 0, 2),
                                             state_conv_c[l].transpose(1, 0, 2), aw, cw, *conv_w)
        af_s = af_s.transpose(1, 0, 2).reshape(n_s, aw)
        cf_s = cf_s.transpose(1, 0, 2).reshape(n_s, cw)

        at_p = _attn_prompt(q, kvb, sb_bias[l], bp, seq, nh, hd, tq, nq, 0, BF16, False, "attn_prompt")
        q_s = q[n_p:].astype(F32).reshape(db, ds, nh // 2, 2, hd)
        zq = jnp.zeros((db, ds, nh // 2, hd), F32)
        q_bd = jnp.concatenate([jnp.concatenate([q_s[:, :, :, 0], zq], axis=-1),
                                jnp.concatenate([zq, q_s[:, :, :, 1]], axis=-1)], axis=1)
        q_bd = q_bd.transpose(0, 2, 1, 3).reshape(db, (nh // 2) * 2 * ds, 2 * hd)
        kn = kv[n_p:, :hw].reshape(db, ds, hw)
        vn = kv[n_p:, hw:].reshape(db, ds, hw)
        raw = _attn_sample(q_bd, kn, vn, cache_k, cache_v, l, page_table, sb_bias[l], ds, nh, hd)
        raw = raw.reshape(db, nh // 2, 2, ds, 2, hd)
        at_s = jnp.stack([raw[:, :, 0, :, 0], raw[:, :, 1, :, 1]], axis=2)
        at_s = at_s.transpose(0, 3, 1, 2, 4).reshape(n_s, hw).astype(BF16)

        merged = _merge((af_p, at_p, cf_p), (af_s, at_s, cf_s), gates, w_out_a[l].astype(BF16),
                        w_out_b[l].astype(BF16), w_out_c[l].astype(BF16), tmx)
        wr = jnp.zeros((d, LANES), F32).at[:, :ng].set(w_router_group[l]).at[:, ng:ng + ne].set(w_router_expert[l])
        br = jnp.zeros((1, LANES), F32).at[0, :ng].set(b_router_group[l]).at[0, ng:ng + ne].set(b_router_expert[l])
        x_in = x
        x, h2, logits = _outproj(merged, x_in, w_o[l].astype(BF16), norm_ffn[l], wr, br, tmx)

        if l < depth - 1:
            x_t = x_in[:n_p].reshape(bp, seq, d)[:, seq - tq:].reshape(bp * tq, d)
            tp = lambda c0, c1, nm: _tail_proj(x_t, norm_mix[l], w_in, l, c0, c1, tn, nm)
            pa_x, q_x = tp(0, c_q, "tail_proj_a"), tp(c_q, c_k, "tail_proj_q")
            pc_x, gl_x = tp(c_c, c_g, "tail_proj_c"), tp(c_g, c_end, "tail_proj_gates")
            af_x, cf_x, _, _ = _mix_prompt(pa_x, pc_x, bp, tq, aw, cw, *conv_w, out_dtype=F32, gate_is_logit=True,
                                           name="tail_mix")
            at_x = _attn_prompt(q_x, kvb, sb_bias[l], bp, seq, nh, hd, tq, 1, nq - 1, F32, True, "tail_attn")
            merged_x = _tail_merge(af_x, at_x, cf_x, gl_x, w_out_a, w_out_b, w_out_c, l, tn)
            logits_x = _tail_router(merged_x, x_t, w_o, l, norm_ffn[l], wr, br, tn)
            keep = tq - 2 * kc
            logits = logits.reshape(-1, LANES)
            for b in range(bp):
                logits = lax.dynamic_update_slice(logits, logits_x[b * tq + keep:(b + 1) * tq],
                                                  (b * seq + seq - tq + keep, 0))

        meta, counts = _route(logits, ng, ne, tmx)
        e1, e2 = meta[:, 0].astype(jnp.int32), meta[:, 1].astype(jnp.int32)
        cnt = counts[0, ng:ng + ne].astype(jnp.int32)
        ends = jnp.cumsum(cnt)
        off = ends - cnt
        pos1 = off[e1] + meta[:, 4].astype(jnp.int32)
        pos2 = off[e2] + meta[:, 5].astype(jnp.int32)
        x_sorted = _dispatch(h2, pos1, pos2)
        starts = jnp.sort(jnp.concatenate([jnp.arange(n_tiles, dtype=jnp.int32) * tme, off]))
        stops = jnp.concatenate([starts[1:], jnp.full((1,), TOP_K * n, jnp.int32)])
        item_tile = jnp.minimum(starts // tme, n_tiles - 1)
        item_expert = jnp.minimum(jnp.searchsorted(ends, starts, side="right"), ne - 1).astype(jnp.int32)
        y_sorted = _experts(x_sorted, item_tile, item_expert, starts - item_tile * tme, stops - item_tile * tme,
                            w_gate, w_up, w_down, l, tme)
        last = l == depth - 1
        x, h = _combine(x, meta, y_sorted, pos1, pos2, norm_final if last else norm_mix[l + 1], F32 if last else BF16)

        outs["kp"].append(kv[:n_p, :hw].reshape(bp, seq, nh, hd))
        outs["vp"].append(kv[:n_p, hw:].reshape(bp, seq, nh, hd))
        outs["ks"].append(kn.reshape(db, ds, nh, hd))
        outs["vs"].append(vn.reshape(db, ds, nh, hd))
        outs["ap"].append(ha_p)
        outs["as"].append(ha_s.transpose(1, 0, 2))
        outs["cp"].append(hc_p)
        outs["cs"].append(hc_s.transpose(1, 0, 2))

    return (h[:n_p].reshape(bp, seq, d), h[n_p:].reshape(db, ds, d),
            jnp.stack(outs["kp"]), jnp.stack(outs["vp"]), jnp.stack(outs["ks"]), jnp.stack(outs["vs"]),
            jnp.stack(outs["ap"]), jnp.stack(outs["as"]), jnp.stack(outs["cp"]), jnp.stack(outs["cs"]))
```

```python
import functools
import math

import jax
import jax.numpy as jnp
from jax import lax
from jax.experimental import pallas as pl
from jax.experimental.pallas import tpu as pltpu

BF16 = jnp.bfloat16
F32 = jnp.float32
RMS_EPS = 1e-6
LN_EPS = 1e-5
TOP_K = 2
LANES = 128
V7X_VMEM_BUDGET = 56 * 1024 * 1024
NEG = -3.0e38
LOG2E = 1.4426950408889634
NT_DIMS = (((1,), (1,)), ((), ()))


def _tile(n, target, align):
    if n <= target:
        return n
    t = (target // align) * align
    while t >= align:
        if n % t == 0:
            return t
        t -= align
    raise ValueError(f"no tile for {n} (target {target}, align {align})")


def _params(*sem):
    return pltpu.CompilerParams(dimension_semantics=sem or None, vmem_limit_bytes=V7X_VMEM_BUDGET)


def _split_bf16(x):
    hi = x.astype(BF16)
    lo = (x - hi.astype(F32)).astype(BF16)
    return hi, lo


def _dot(a, b):
    return jnp.dot(a, b, preferred_element_type=F32)


def _dot3(a_hi, a_lo, w):
    w_hi, w_lo = _split_bf16(w)
    return _dot(a_hi, w_hi) + _dot(a_hi, w_lo) + _dot(a_lo, w_hi)


def _rms(x, g):
    ms = jnp.mean(x * x, axis=-1, keepdims=True)
    return x * lax.rsqrt(ms + RMS_EPS) * g


def _rmsnorm_kernel(x_ref, g_ref, o_ref):
    o_ref[...] = _rms(x_ref[...], g_ref[...]).astype(o_ref.dtype)


def _rmsnorm(x, g, out_dtype):
    n, d = x.shape
    tm = _tile(n, 512, 16)
    return pl.pallas_call(
        _rmsnorm_kernel,
        out_shape=jax.ShapeDtypeStruct((n, d), out_dtype),
        grid=(n // tm,),
        in_specs=[pl.BlockSpec((tm, d), lambda i: (i, 0)), pl.BlockSpec((1, d), lambda i: (0, 0))],
        out_specs=pl.BlockSpec((tm, d), lambda i: (i, 0)),
        compiler_params=_params("arbitrary"),
        name="rmsnorm",
    )(x, g.reshape(1, d))


def _proj_kernel(h_ref, w_ref, *o_refs, sigmoid_from):
    acc = _dot(h_ref[...], w_ref[...].astype(BF16))

    def store(val):
        for o_ref in o_refs:
            o_ref[...] = val.astype(o_ref.dtype)

    if sigmoid_from is None:
        store(acc)
    else:
        j = pl.program_id(1)

        @pl.when(j >= sigmoid_from)
        def _():
            store(jax.nn.sigmoid(acc))

        @pl.when(j < sigmoid_from)
        def _():
            store(acc)


def _proj(h, w_in, layer, c0, c1, tn, out_dtypes, sigmoid_from=None, name="proj"):
    n, k = h.shape
    tm = _tile(n, 2176, 16)
    cb0 = c0 // tn
    outs = pl.pallas_call(
        functools.partial(_proj_kernel, sigmoid_from=sigmoid_from),
        out_shape=tuple(jax.ShapeDtypeStruct((n, c1 - c0), dt) for dt in out_dtypes),
        grid=(n // tm, (c1 - c0) // tn),
        in_specs=[pl.BlockSpec((tm, k), lambda i, j: (i, 0)),
                  pl.BlockSpec((None, k, tn), lambda i, j: (layer, 0, cb0 + j))],
        out_specs=tuple(pl.BlockSpec((tm, tn), lambda i, j: (i, j)) for _ in out_dtypes),
        compiler_params=_params("arbitrary", "arbitrary"),
        name=name,
    )(h, w_in)
    return outs if len(out_dtypes) > 1 else outs[0]


def _tail_proj_kernel(x_ref, g_ref, w_ref, o_ref, hh_ref, hl_ref):
    @pl.when(pl.program_id(0) == 0)
    def _():
        hi, lo = _split_bf16(_rms(x_ref[...], g_ref[...]))
        hh_ref[...] = hi
        hl_ref[...] = lo

    o_ref[...] = _dot3(hh_ref[...], hl_ref[...], w_ref[...])


def _tail_proj(x_t, g, w_in, layer, c0, c1, tn, name):
    r, d = x_t.shape
    cb0 = c0 // tn
    return pl.pallas_call(
        _tail_proj_kernel,
        out_shape=jax.ShapeDtypeStruct((r, c1 - c0), F32),
        grid=((c1 - c0) // tn,),
        in_specs=[pl.BlockSpec((r, d), lambda j: (0, 0)), pl.BlockSpec((1, d), lambda j: (0, 0)),
                  pl.BlockSpec((None, d, tn), lambda j: (layer, 0, cb0 + j))],
        out_specs=pl.BlockSpec((r, tn), lambda j: (0, j)),
        scratch_shapes=[pltpu.VMEM((r, d), BF16), pltpu.VMEM((r, d), BF16)],
        compiler_params=_params("arbitrary"),
        name=name,
    )(x_t, g.reshape(1, d), w_in)


def _ln_silu(acc, g, b):
    mu = jnp.mean(acc, axis=-1, keepdims=True)
    xc = acc - mu
    var = jnp.mean(xc * xc, axis=-1, keepdims=True)
    y = xc * lax.rsqrt(var + LN_EPS) * g + b
    return y * jax.nn.sigmoid(y)


def _mix_prompt_kernel(ab_ref, ac_ref, ax_ref, cv_ref, cg_ref, caw_ref, ccw_ref, ccb_ref, lng_ref, lnb_ref,
                       af_ref, cf_ref, ha_ref, hc_ref, exa_ref, exc_ref, *, tt, ka, kc, pa, pc, gate_is_logit):
    ti = pl.program_id(1)

    @pl.when(ti == 0)
    def _():
        exa_ref[0:pa, :] = jnp.zeros((pa, exa_ref.shape[1]), F32)
        exc_ref[0:pc, :] = jnp.zeros((pc, exc_ref.shape[1]), F32)

    u = ac_ref[...].astype(F32) * ax_ref[...].astype(F32)
    exa_ref[pa:pa + tt, :] = u
    conv = caw_ref[ka - 1:ka, :] * u
    for j in range(ka - 1):
        conv = conv + caw_ref[j:j + 1, :] * exa_ref[pa - (ka - 1) + j:pa - (ka - 1) + j + tt, :]
    af_ref[...] = (ab_ref[...].astype(F32) * conv).astype(af_ref.dtype)

    cg = cg_ref[...].astype(F32)
    glu = cv_ref[...].astype(F32) * (jax.nn.sigmoid(cg) if gate_is_logit else cg)
    exc_ref[pc:pc + tt, :] = glu
    acc = ccb_ref[...] + ccw_ref[kc - 1:kc, :] * glu
    for j in range(kc - 1):
        acc = acc + ccw_ref[j:j + 1, :] * exc_ref[pc - (kc - 1) + j:pc - (kc - 1) + j + tt, :]
    cf_ref[...] = _ln_silu(acc, lng_ref[...], lnb_ref[...]).astype(cf_ref.dtype)

    @pl.when(ti == pl.num_programs(1) - 1)
    def _():
        ha_ref[...] = exa_ref[pa + tt - (ka - 1):pa + tt, :]
        hc_ref[...] = exc_ref[pc + tt - (kc - 1):pc + tt, :]

    exa_ref[0:pa, :] = exa_ref[tt:tt + pa, :]
    exc_ref[0:pc, :] = exc_ref[tt:tt + pc, :]


def _mix_prompt(pa_arr, pc_arr, bp, seq, aw, cw, caw, ccw, ccb, lng, lnb, out_dtype=None, gate_is_logit=False,
                name="mix_prompt"):
    out_dtype = out_dtype or BF16
    ka, kc = caw.shape[0], ccw.shape[0]
    tt = _tile(seq, 512, 32)
    nt = seq // tt
    pa, pc = 8, 32
    assert ka - 1 <= pa and kc - 1 <= pc and tt >= pc
    row = lambda c: (lambda b, t: (b * nt + t, c))
    const = lambda b, t: (0, 0)
    return pl.pallas_call(
        functools.partial(_mix_prompt_kernel, tt=tt, ka=ka, kc=kc, pa=pa, pc=pc, gate_is_logit=gate_is_logit),
        out_shape=(jax.ShapeDtypeStruct((bp * seq, aw), out_dtype), jax.ShapeDtypeStruct((bp * seq, cw), out_dtype),
                   jax.ShapeDtypeStruct((bp, ka - 1, aw), F32), jax.ShapeDtypeStruct((bp, kc - 1, cw), F32)),
        grid=(bp, nt),
        in_specs=[pl.BlockSpec((tt, aw), row(0)), pl.BlockSpec((tt, aw), row(1)), pl.BlockSpec((tt, aw), row(2)),
                  pl.BlockSpec((tt, cw), row(0)), pl.BlockSpec((tt, cw), row(1)),
                  pl.BlockSpec((ka, aw), const), pl.BlockSpec((kc, cw), const), pl.BlockSpec((1, cw), const),
                  pl.BlockSpec((1, cw), const), pl.BlockSpec((1, cw), const)],
        out_specs=(pl.BlockSpec((tt, aw), row(0)), pl.BlockSpec((tt, cw), row(0)),
                   pl.BlockSpec((None, ka - 1, aw), lambda b, t: (b, 0, 0)),
                   pl.BlockSpec((None, kc - 1, cw), lambda b, t: (b, 0, 0))),
        scratch_shapes=[pltpu.VMEM((pa + tt, aw), F32), pltpu.VMEM((pc + tt, cw), F32)],
        compiler_params=_params("arbitrary", "arbitrary"),
        name=name,
    )(pa_arr, pa_arr, pa_arr, pc_arr, pc_arr, caw, ccw, ccb.reshape(1, cw), lng.reshape(1, cw), lnb.reshape(1, cw))


def _mix_sample_kernel(pa_ref, pc_ref, ha_ref, hc_ref, caw_ref, ccw_ref, ccb_ref, lng_ref, lnb_ref,
                       af_ref, cf_ref, nha_ref, nhc_ref, *, aw, cw):
    ds = pa_ref.shape[0]
    ka, kc = caw_ref.shape[0], ccw_ref.shape[0]
    u = [pa_ref[t, :, aw:2 * aw].astype(F32) * pa_ref[t, :, 2 * aw:3 * aw].astype(F32) for t in range(ds)]
    glu = [pc_ref[t, :, 0:cw].astype(F32) * pc_ref[t, :, cw:2 * cw].astype(F32) for t in range(ds)]
    full_a = lambda i: ha_ref[i] if i < ka - 1 else u[i - (ka - 1)]
    full_c = lambda i: hc_ref[i] if i < kc - 1 else glu[i - (kc - 1)]
    for t in range(ds):
        conv = caw_ref[0:1, :] * full_a(t)
        for j in range(1, ka):
            conv = conv + caw_ref[j:j + 1, :] * full_a(t + j)
        af_ref[t] = (pa_ref[t, :, 0:aw].astype(F32) * conv).astype(af_ref.dtype)
        acc = ccb_ref[...] + ccw_ref[0:1, :] * full_c(t)
        for j in range(1, kc):
            acc = acc + ccw_ref[j:j + 1, :] * full_c(t + j)
        cf_ref[t] = _ln_silu(acc, lng_ref[...], lnb_ref[...]).astype(cf_ref.dtype)
    for i in range(ka - 1):
        nha_ref[i] = full_a(i + ds)
    for i in range(kc - 1):
        nhc_ref[i] = full_c(i + ds)


def _mix_sample(pa_t, pc_t, ha_t, hc_t, aw, cw, caw, ccw, ccb, lng, lnb):
    ds, db, _ = pa_t.shape
    ka, kc = caw.shape[0], ccw.shape[0]
    return pl.pallas_call(
        functools.partial(_mix_sample_kernel, aw=aw, cw=cw),
        out_shape=(jax.ShapeDtypeStruct((ds, db, aw), BF16), jax.ShapeDtypeStruct((ds, db, cw), BF16),
                   jax.ShapeDtypeStruct((ka - 1, db, aw), F32), jax.ShapeDtypeStruct((kc - 1, db, cw), F32)),
        compiler_params=_params(),
        name="mix_sample",
    )(pa_t, pc_t, ha_t, hc_t, caw, ccw, ccb.reshape(1, cw), lng.reshape(1, cw), lnb.reshape(1, cw))


def _suffix_matrix(bk):
    j = lax.broadcasted_iota(jnp.int32, (bk, bk), 0)
    s = lax.broadcasted_iota(jnp.int32, (bk, bk), 1)
    return (j >= s).astype(BF16)


def _sb_block(z2, mask, tmat, carry, two_pass):
    bk = z2.shape[1]
    sp = jnp.maximum(z2, 0.0) + jnp.log2(1.0 + jnp.exp2(-jnp.abs(z2)))
    if mask is not None:
        sp = jnp.where(mask, sp, 0.0)
    if two_pass:
        hi, lo = _split_bf16(sp)
        suf = _dot(hi, tmat) + _dot(lo, tmat)
    else:
        suf = _dot(sp.astype(BF16), tmat)
    c = suf + jnp.concatenate([carry] * (bk // LANES), axis=1)
    a = jnp.exp2(z2 - c)
    if mask is not None:
        a = jnp.where(mask, a, 0.0)
    return a, carry + jnp.broadcast_to(suf[:, 0:1], carry.shape)


def _attn_prompt_kernel(bias_ref, q_ref, k_ref, v_ref, tmat_ref, o_ref, acc_ref, carry_ref,
                        *, tq, hd, hg, qi_base, scale, precise):
    g = pl.program_id(1)
    qi = qi_base + pl.program_id(2)
    tmat = tmat_ref[...]

    qs = []
    for h in range(hg):
        qh = q_ref[:, h * hd:(h + 1) * hd]
        qs.append(_split_bf16(qh) if precise else (qh.astype(BF16),))

    def logits(h, start):
        kblk = k_ref[pl.ds(start, tq), h * hd:(h + 1) * hd]
        s = sum(lax.dot_general(part, kblk, NT_DIMS, preferred_element_type=F32) for part in qs[h])
        return s * (scale * LOG2E) + bias_ref[g * hg + h] * LOG2E

    d0 = pl.multiple_of(qi * tq, tq)
    r = lax.broadcasted_iota(jnp.int32, (tq, tq), 0)
    c = lax.broadcasted_iota(jnp.int32, (tq, tq), 1)
    for h in range(hg):
        a, carry = _sb_block(logits(h, d0), c < r, tmat, jnp.zeros((tq, LANES), F32), precise)
        acc_ref[:, h * hd:(h + 1) * hd] = _dot(a.astype(BF16), v_ref[pl.ds(d0, tq), h * hd:(h + 1) * hd])
        carry_ref[h] = carry

    def body(it, _):
        s0 = pl.multiple_of((qi - 1 - it) * tq, tq)
        for h in range(hg):
            a, carry = _sb_block(logits(h, s0), None, tmat, carry_ref[h], precise)
            acc_ref[:, h * hd:(h + 1) * hd] += _dot(a.astype(BF16), v_ref[pl.ds(s0, tq), h * hd:(h + 1) * hd])
            carry_ref[h] = carry
        return 0

    lax.fori_loop(0, qi, body, 0)
    o_ref[...] = acc_ref[...].astype(o_ref.dtype)


def _attn_prompt(q, kvb, bias, bp, seq, nh, hd, tq, nq, qi_base, out_dtype, precise, name):
    hg = 4 if nh % 4 == 0 else 2
    ng = nh // hg
    return pl.pallas_call(
        functools.partial(_attn_prompt_kernel, tq=tq, hd=hd, hg=hg, qi_base=qi_base, scale=hd ** -0.5,
                          precise=precise),
        out_shape=jax.ShapeDtypeStruct((bp * nq * tq, nh * hd), out_dtype),
        grid=(bp, ng, nq),
        in_specs=[pl.BlockSpec(memory_space=pltpu.SMEM),
                  pl.BlockSpec((tq, hg * hd), lambda b, g, i: (b * nq + i, g)),
                  pl.BlockSpec((seq, hg * hd), lambda b, g, i: (b, g)),
                  pl.BlockSpec((seq, hg * hd), lambda b, g, i: (b, ng + g)),
                  pl.BlockSpec((tq, tq), lambda b, g, i: (0, 0))],
        out_specs=pl.BlockSpec((tq, hg * hd), lambda b, g, i: (b * nq + i, g)),
        scratch_shapes=[pltpu.VMEM((tq, hg * hd), F32), pltpu.VMEM((hg, tq, LANES), F32)],
        compiler_params=_params("arbitrary", "arbitrary", "arbitrary"),
        name=name,
    )(bias, q, kvb, kvb, _suffix_matrix(tq))


def _attn_sample_kernel(pt_ref, bias_ref, q_ref, kn_ref, vn_ref, tmat_ref, ck_hbm, cv_hbm, o_ref,
                        knf_ref, vnf_ref, kbuf, vbuf, sem, *, layer, npg, ds, hd, nh, page, scale):
    n = pl.program_id(0)
    rr, ww = 2 * ds, 2 * hd
    p2 = nh // 2
    rows = p2 * rr

    def page_copies(seq_idx, slot):
        for pg in range(npg):
            src = pt_ref[seq_idx * npg + pg]
            yield pltpu.make_async_copy(ck_hbm.at[layer, src], kbuf.at[slot, pg], sem.at[slot])
            yield pltpu.make_async_copy(cv_hbm.at[layer, src], vbuf.at[slot, pg], sem.at[slot])

    @pl.when(n == 0)
    def _():
        for cp in page_copies(0, 0):
            cp.start()

    @pl.when(n + 1 < pl.num_programs(0))
    def _():
        for cp in page_copies(n + 1, (n + 1) % 2):
            cp.start()

    slot = n % 2
    for cp in page_copies(n, slot):
        cp.wait()

    knf_ref[...] = jnp.zeros(knf_ref.shape, F32)
    vnf_ref[...] = jnp.zeros(vnf_ref.shape, F32)
    knf_ref[0:ds, :] = kn_ref[...]
    vnf_ref[0:ds, :] = vn_ref[...]

    row = lax.broadcasted_iota(jnp.int32, (rows, LANES), 0)
    col = lax.broadcasted_iota(jnp.int32, (rows, LANES), 1)
    head = (row // rr) * 2 + (row % rr) // ds
    bias = jnp.zeros((rows, LANES), F32)
    for hh in range(nh):
        bias = jnp.where(head == hh, bias_ref[hh] * LOG2E, bias)
    new_mask = col < (row % ds)

    lhs = [q_ref[p * rr:(p + 1) * rr, :].astype(BF16) for p in range(p2)]

    def new_tile(ref, p):
        return ref[:, p * ww:(p + 1) * ww].astype(BF16)

    def page_tile(buf, pages, p):
        per_page = [jnp.concatenate([buf.at[slot, pg][pl.ds(2 * p + e, page, stride=nh), :] for e in range(2)], axis=1)
                    for pg in pages]
        return jnp.concatenate(per_page, axis=0).astype(BF16)

    def block(k_tile, v_tile, nkeys, mask, carry, acc):
        s = jnp.concatenate(
            [lax.dot_general(lhs[p], k_tile(p), NT_DIMS, preferred_element_type=F32) for p in range(p2)], axis=0)
        z2 = s * (scale * LOG2E) + jnp.concatenate([bias] * (nkeys // LANES), axis=1)
        a, carry = _sb_block(z2, mask, tmat_ref[0:nkeys, 0:nkeys], carry, True)
        acc = [acc[p] + _dot(a[p * rr:(p + 1) * rr, :].astype(BF16), v_tile(p)) for p in range(p2)]
        return carry, acc

    carry = jnp.zeros((rows, LANES), F32)
    acc = [jnp.zeros((rr, ww), F32) for _ in range(p2)]
    carry, acc = block(functools.partial(new_tile, knf_ref), functools.partial(new_tile, vnf_ref), page, new_mask,
                       carry, acc)
    hi_pg = npg
    while hi_pg > 0:
        pages = list(range(max(hi_pg - 2, 0), hi_pg))
        carry, acc = block(functools.partial(page_tile, kbuf, pages), functools.partial(page_tile, vbuf, pages),
                           len(pages) * page, None, carry, acc)
        hi_pg -= len(pages)
    for p in range(p2):
        o_ref[p * rr:(p + 1) * rr, :] = acc[p]


def _attn_sample(q_bd, kn, vn, cache_k, cache_v, layer, page_table, bias, ds, nh, hd):
    db = q_bd.shape[0]
    npg = page_table.shape[1]
    depth, npool, page = cache_k.shape[:3]
    hw = nh * hd
    assert page == LANES and hd == LANES and (2 * ds) % 8 == 0 and nh % 2 == 0
    ck = cache_k.reshape(depth, npool, page * nh, hd)
    cv = cache_v.reshape(depth, npool, page * nh, hd)
    rows = (nh // 2) * 2 * ds
    seq_blk = lambda shape: pl.BlockSpec((None,) + shape, lambda n, pt: (n, 0, 0))
    grid_spec = pltpu.PrefetchScalarGridSpec(
        num_scalar_prefetch=1,
        grid=(db,),
        in_specs=[pl.BlockSpec(memory_space=pltpu.SMEM), seq_blk((rows, 2 * hd)), seq_blk((ds, hw)), seq_blk((ds, hw)),
                  pl.BlockSpec((2 * page, 2 * page), lambda n, pt: (0, 0)),
                  pl.BlockSpec(memory_space=pl.ANY), pl.BlockSpec(memory_space=pl.ANY)],
        out_specs=seq_blk((rows, 2 * hd)),
        scratch_shapes=[pltpu.VMEM((page, hw), F32), pltpu.VMEM((page, hw), F32),
                        pltpu.VMEM((2, npg, page * nh, hd), F32), pltpu.VMEM((2, npg, page * nh, hd), F32),
                        pltpu.SemaphoreType.DMA((2,))],
    )
    return pl.pallas_call(
        functools.partial(_attn_sample_kernel, layer=layer, npg=npg, ds=ds, hd=hd, nh=nh, page=page,
                          scale=hd ** -0.5),
        out_shape=jax.ShapeDtypeStruct((db, rows, 2 * hd), F32),
        grid_spec=grid_spec,
        compiler_params=_params("arbitrary"),
        name="attn_sample",
    )(page_table.reshape(-1), bias, q_bd, kn, vn, _suffix_matrix(2 * page), ck, cv)


def _merge_kernel(afp_ref, atp_ref, cfp_ref, afs_ref, ats_ref, cfs_ref, g0_ref, g1_ref, g2_ref,
                  wa_ref, wb_ref, wc_ref, o_ref, *, n_prompt_tiles):
    def run(af_ref, at_ref, cf_ref):
        ya = _dot(af_ref[...], wa_ref[...])
        yb = _dot(at_ref[...], wb_ref[...])
        yc = _dot(cf_ref[...], wc_ref[...])
        o_ref[...] = (g0_ref[...].astype(F32) * ya + g1_ref[...].astype(F32) * yb
                      + g2_ref[...].astype(F32) * yc).astype(o_ref.dtype)

    i = pl.program_id(0)

    @pl.when(i < n_prompt_tiles)
    def _():
        run(afp_ref, atp_ref, cfp_ref)

    @pl.when(i >= n_prompt_tiles)
    def _():
        run(afs_ref, ats_ref, cfs_ref)


def _merge(feats_p, feats_s, gates, wa, wb, wc, tm):
    n_p, n_s = feats_p[0].shape[0], feats_s[0].shape[0]
    d = wa.shape[1]
    npt, nst = n_p // tm, n_s // tm
    prow = lambda a: pl.BlockSpec((tm, a.shape[1]), lambda i: (jnp.minimum(i, npt - 1), 0))
    srow = lambda a: pl.BlockSpec((tm, a.shape[1]), lambda i: (jnp.maximum(i - npt, 0), 0))
    gate = lambda k: pl.BlockSpec((tm, d), lambda i: (i, k))
    resident = lambda w: pl.BlockSpec(w.shape, lambda i: (0, 0), pipeline_mode=pl.Buffered(1))
    return pl.pallas_call(
        functools.partial(_merge_kernel, n_prompt_tiles=npt),
        out_shape=jax.ShapeDtypeStruct((n_p + n_s, d), BF16),
        grid=(npt + nst,),
        in_specs=[prow(a) for a in feats_p] + [srow(a) for a in feats_s] + [gate(0), gate(1), gate(2),
                  resident(wa), resident(wb), resident(wc)],
        out_specs=pl.BlockSpec((tm, d), lambda i: (i, 0)),
        compiler_params=_params("arbitrary"),
        name="merge",
    )(*feats_p, *feats_s, gates, gates, gates, wa, wb, wc)


def _router_logits(h, wr, br):
    hh, hl = _split_bf16(h)
    return _dot3(hh, hl, wr) + br


def _outproj_kernel(m_ref, x_ref, wo_ref, g_ref, wr_ref, br_ref, xo_ref, h_ref, lg_ref):
    xn = x_ref[...] + _dot(m_ref[...], wo_ref[...])
    xo_ref[...] = xn
    h = _rms(xn, g_ref[...])
    h_ref[...] = h
    lg_ref[...] = _router_logits(h, wr_ref[...], br_ref[...])


def _outproj(merged, x, wo, g, wr, br, tm):
    n, d = x.shape
    rowb = lambda w: pl.BlockSpec((tm, w), lambda i: (i, 0))
    const = lambda shape: pl.BlockSpec(shape, lambda i: (0, 0))
    return pl.pallas_call(
        _outproj_kernel,
        out_shape=(jax.ShapeDtypeStruct((n, d), F32), jax.ShapeDtypeStruct((n, d), F32),
                   jax.ShapeDtypeStruct((n, LANES), F32)),
        grid=(n // tm,),
        in_specs=[rowb(d), rowb(d), pl.BlockSpec((d, d), lambda i: (0, 0), pipeline_mode=pl.Buffered(1)),
                  const((1, d)), const((d, LANES)), const((1, LANES))],
        out_specs=(rowb(d), rowb(d), rowb(LANES)),
        compiler_params=_params("arbitrary"),
        name="outproj",
    )(merged, x, wo, g.reshape(1, d), wr, br)


def _tail_merge_kernel(af_ref, at_ref, cf_ref, g0_ref, g1_ref, g2_ref, wa_ref, wb_ref, wc_ref, o_ref):
    y = [_dot3(*_split_bf16(f_ref[...]), w_ref[...])
         for f_ref, w_ref in ((af_ref, wa_ref), (at_ref, wb_ref), (cf_ref, wc_ref))]
    o_ref[...] = (jax.nn.sigmoid(g0_ref[...]) * y[0] + jax.nn.sigmoid(g1_ref[...]) * y[1]
                  + jax.nn.sigmoid(g2_ref[...]) * y[2])


def _tail_merge(af, at, cf, gate_logits, wa, wb, wc, layer, tn):
    r, d = af.shape[0], wa.shape[-1]
    full = lambda a: pl.BlockSpec(a.shape, lambda j: (0, 0))
    gate = lambda k: pl.BlockSpec((r, tn), lambda j: (0, k * (d // tn) + j))
    wcol = lambda w: pl.BlockSpec((None, w.shape[1], tn), lambda j: (layer, 0, j))
    return pl.pallas_call(
        _tail_merge_kernel,
        out_shape=jax.ShapeDtypeStruct((r, d), F32),
        grid=(d // tn,),
        in_specs=[full(af), full(at), full(cf), gate(0), gate(1), gate(2), wcol(wa), wcol(wb), wcol(wc)],
        out_specs=pl.BlockSpec((r, tn), lambda j: (0, j)),
        compiler_params=_params("arbitrary"),
        name="tail_merge",
    )(af, at, cf, gate_logits, gate_logits, gate_logits, wa, wb, wc)


def _tail_router_kernel(m_ref, x_ref, wo_ref, g_ref, wr_ref, br_ref, lg_ref, mh_ref, ml_ref, xn_ref, *, tn):
    j = pl.program_id(0)

    @pl.when(j == 0)
    def _():
        hi, lo = _split_bf16(m_ref[...])
        mh_ref[...] = hi
        ml_ref[...] = lo

    col = pl.multiple_of(j * tn, tn)
    xn_ref[:, pl.ds(col, tn)] = x_ref[...] + _dot3(mh_ref[...], ml_ref[...], wo_ref[...])

    @pl.when(j == pl.num_programs(0) - 1)
    def _():
        lg_ref[...] = _router_logits(_rms(xn_ref[...], g_ref[...]), wr_ref[...], br_ref[...])


def _tail_router(merged, x_t, wo, layer, g, wr, br, tn):
    r, d = x_t.shape
    const = lambda shape: pl.BlockSpec(shape, lambda j: (0, 0))
    return pl.pallas_call(
        functools.partial(_tail_router_kernel, tn=tn),
        out_shape=jax.ShapeDtypeStruct((r, LANES), F32),
        grid=(d // tn,),
        in_specs=[const((r, d)), pl.BlockSpec((r, tn), lambda j: (0, j)),
                  pl.BlockSpec((None, d, tn), lambda j: (layer, 0, j)), const((1, d)), const((d, LANES)),
                  const((1, LANES))],
        out_specs=const((r, LANES)),
        scratch_shapes=[pltpu.VMEM((r, d), BF16), pltpu.VMEM((r, d), BF16), pltpu.VMEM((r, d), F32)],
        compiler_params=_params("arbitrary"),
        name="tail_router",
    )(merged, x_t, wo, g.reshape(1, d), wr, br)


def _route_kernel(lg_ref, tri_ref, meta_ref, cnt_ref, run_ref, *, ng, ne):
    i = pl.program_id(0)

    @pl.when(i == 0)
    def _():
        run_ref[...] = jnp.zeros(run_ref.shape, F32)

    x = lg_ref[...]
    epg = ne // ng
    lane = lax.broadcasted_iota(jnp.int32, x.shape, 1).astype(F32)
    rmax = lambda v: jnp.max(v, axis=1, keepdims=True)
    rmin = lambda v: jnp.min(v, axis=1, keepdims=True)
    rsum = lambda v: jnp.sum(v, axis=1, keepdims=True)

    is_g = lane < ng
    gl = jnp.where(is_g, x, NEG)
    gmax = rmax(gl)
    gidx = rmin(jnp.where(is_g & (gl == gmax), lane, float(LANES)))
    gw = 1.0 / rsum(jnp.where(is_g, jnp.exp(jnp.minimum(x - gmax, 0.0)), 0.0))

    lo = ng + gidx * epg
    in_grp = (lane >= lo) & (lane < lo + epg)
    el = jnp.where(in_grp, x, NEG)
    v1 = rmax(el)
    i1 = rmin(jnp.where(in_grp & (el == v1), lane, float(LANES)))
    rest = in_grp & (lane != i1)
    el2 = jnp.where(rest, x, NEG)
    v2 = rmax(el2)
    i2 = rmin(jnp.where(rest & (el2 == v2), lane, float(LANES)))
    t = jnp.exp(v2 - v1)
    w1 = gw / (1.0 + t)
    w2 = gw * t / (1.0 + t)

    sel1, sel2 = lane == i1, lane == i2
    onehot = (sel1 | sel2).astype(BF16)
    rank = _dot(tri_ref[...], onehot) + run_ref[...]
    r1 = rsum(jnp.where(sel1, rank, 0.0))
    r2 = rsum(jnp.where(sel2, rank, 0.0))
    run_ref[...] += jnp.sum(onehot.astype(F32), axis=0, keepdims=True)
    cnt_ref[...] = run_ref[...]

    vals = [i1 - ng, i2 - ng, w1, w2, r1, r2]
    meta = jnp.zeros(x.shape, F32)
    for k, v in enumerate(vals):
        meta = jnp.where(lane == k, v, meta)
    meta_ref[...] = meta


def _route(logits, ng, ne, tr):
    n = logits.shape[0]
    r = lax.broadcasted_iota(jnp.int32, (tr, tr), 0)
    c = lax.broadcasted_iota(jnp.int32, (tr, tr), 1)
    tri = (c < r).astype(BF16)
    return pl.pallas_call(
        functools.partial(_route_kernel, ng=ng, ne=ne),
        out_shape=(jax.ShapeDtypeStruct((n, LANES), F32), jax.ShapeDtypeStruct((1, LANES), F32)),
        grid=(n // tr,),
        in_specs=[pl.BlockSpec((tr, LANES), lambda i: (i, 0)), pl.BlockSpec((tr, tr), lambda i: (0, 0))],
        out_specs=(pl.BlockSpec((tr, LANES), lambda i: (i, 0)), pl.BlockSpec((1, LANES), lambda i: (0, 0))),
        scratch_shapes=[pltpu.VMEM((1, LANES), F32)],
        compiler_params=_params("arbitrary"),
        name="route",
    )(logits, tri)


def _dispatch_kernel(p1_ref, p2_ref, h_ref, xs_hbm, sem, *, tc):
    i = pl.program_id(0)

    def row(r, carry):
        t = i * tc + r
        pltpu.make_async_copy(h_ref.at[pl.ds(r, 1)], xs_hbm.at[pl.ds(p1_ref[t], 1)], sem).start()
        pltpu.make_async_copy(h_ref.at[pl.ds(r, 1)], xs_hbm.at[pl.ds(p2_ref[t], 1)], sem).start()
        return carry

    lax.fori_loop(0, tc, row, 0, unroll=8)
    for _ in range(TOP_K):
        pltpu.make_async_copy(h_ref, xs_hbm.at[pl.ds(0, tc)], sem).wait()


def _dispatch(h2, pos1, pos2):
    n, d = h2.shape
    tc = _tile(n, 256, 8)
    grid_spec = pltpu.PrefetchScalarGridSpec(
        num_scalar_prefetch=2,
        grid=(n // tc,),
        in_specs=[pl.BlockSpec((tc, d), lambda i, a, b: (i, 0))],
        out_specs=pl.BlockSpec(memory_space=pl.ANY),
        scratch_shapes=[pltpu.SemaphoreType.DMA(())],
    )
    return pl.pallas_call(
        functools.partial(_dispatch_kernel, tc=tc),
        out_shape=jax.ShapeDtypeStruct((TOP_K * n, d), h2.dtype),
        grid_spec=grid_spec,
        compiler_params=_params("arbitrary"),
        name="dispatch",
    )(pos1, pos2, h2)


def _expert_kernel(it_ref, ie_ref, lo_ref, hi_ref, x_ref, wg_ref, wu_ref, wd_ref, y_ref, wgb, wub, wdb, cur_ref):
    g = pl.program_id(0)
    first_of_tile = (g == 0) | (it_ref[g] != it_ref[jnp.maximum(g - 1, 0)])
    lo, hi = lo_ref[g], hi_ref[g]

    @pl.when(g == 0)
    def _():
        cur_ref[0] = -1

    @pl.when(hi > lo)
    def _():
        @pl.when(cur_ref[0] != ie_ref[g])
        def _():
            wgb[...] = wg_ref[...].astype(BF16)
            wub[...] = wu_ref[...].astype(BF16)
            wdb[...] = wd_ref[...].astype(BF16)
            cur_ref[0] = ie_ref[g]

        x = x_ref[...].astype(BF16)
        gate = _dot(x, wgb[...])
        act = gate * jax.nn.sigmoid(gate) * _dot(x, wub[...])
        y = _dot(act.astype(BF16), wdb[...])
        row = lax.broadcasted_iota(jnp.int32, y.shape, 0)
        y = jnp.where((row >= lo) & (row < hi), y, 0.0)

        @pl.when(first_of_tile)
        def _():
            y_ref[...] = y

        @pl.when(jnp.logical_not(first_of_tile))
        def _():
            y_ref[...] += y

    @pl.when((hi <= lo) & first_of_tile)
    def _():
        y_ref[...] = jnp.zeros(y_ref.shape, F32)


def _experts(x_sorted, item_tile, item_expert, item_lo, item_hi, w_gate, w_up, w_down, layer, tm):
    rows, d = x_sorted.shape
    f = w_gate.shape[-1]
    wspec = lambda a, b: pl.BlockSpec((None, None, a, b), lambda g, it, ie, lo, hi: (layer, ie[g], 0, 0))
    grid_spec = pltpu.PrefetchScalarGridSpec(
        num_scalar_prefetch=4,
        grid=(item_tile.shape[0],),
        in_specs=[pl.BlockSpec((tm, d), lambda g, it, ie, lo, hi: (it[g], 0)), wspec(d, f), wspec(d, f), wspec(f, d)],
        out_specs=pl.BlockSpec((tm, d), lambda g, it, ie, lo, hi: (it[g], 0)),
        scratch_shapes=[pltpu.VMEM((d, f), BF16), pltpu.VMEM((d, f), BF16), pltpu.VMEM((f, d), BF16),
                        pltpu.SMEM((1,), jnp.int32)],
    )
    return pl.pallas_call(
        _expert_kernel,
        out_shape=jax.ShapeDtypeStruct((rows, d), F32),
        grid_spec=grid_spec,
        compiler_params=_params("arbitrary"),
        name="experts",
    )(item_tile, item_expert, item_lo, item_hi, x_sorted, w_gate, w_up, w_down)


def _combine_kernel(p1_ref, p2_ref, x_ref, meta_ref, y_hbm, g_ref, xo_ref, ho_ref, ybuf, sem, *, tc):
    i = pl.program_id(0)
    nsteps = pl.num_programs(0)

    def start_gather(tile, slot):
        def body(r, carry):
            t = tile * tc + r
            pltpu.make_async_copy(y_hbm.at[pl.ds(p1_ref[t], 1)], ybuf.at[slot, 0, pl.ds(r, 1)], sem.at[slot]).start()
            pltpu.make_async_copy(y_hbm.at[pl.ds(p2_ref[t], 1)], ybuf.at[slot, 1, pl.ds(r, 1)], sem.at[slot]).start()
            return carry
        lax.fori_loop(0, tc, body, 0, unroll=8)

    @pl.when(i == 0)
    def _():
        start_gather(0, 0)

    @pl.when(i + 1 < nsteps)
    def _():
        start_gather(i + 1, (i + 1) % 2)

    slot = i % 2
    for k in range(TOP_K):
        pltpu.make_async_copy(y_hbm.at[pl.ds(0, tc)], ybuf.at[slot, k], sem.at[slot]).wait()
    xo = x_ref[...] + (meta_ref[:, 2:3] * ybuf[slot, 0] + meta_ref[:, 3:4] * ybuf[slot, 1])
    xo_ref[...] = xo
    ho_ref[...] = _rms(xo, g_ref[...]).astype(ho_ref.dtype)


def _combine(x, meta, y_sorted, pos1, pos2, g_next, h_dtype):
    n, d = x.shape
    tc = _tile(n, 256, 16)
    grid_spec = pltpu.PrefetchScalarGridSpec(
        num_scalar_prefetch=2,
        grid=(n // tc,),
        in_specs=[pl.BlockSpec((tc, d), lambda i, a, b: (i, 0)), pl.BlockSpec((tc, LANES), lambda i, a, b: (i, 0)),
                  pl.BlockSpec(memory_space=pl.ANY), pl.BlockSpec((1, d), lambda i, a, b: (0, 0))],
        out_specs=(pl.BlockSpec((tc, d), lambda i, a, b: (i, 0)), pl.BlockSpec((tc, d), lambda i, a, b: (i, 0))),
        scratch_shapes=[pltpu.VMEM((2, TOP_K, tc, d), F32), pltpu.SemaphoreType.DMA((2,))],
    )
    return pl.pallas_call(
        functools.partial(_combine_kernel, tc=tc),
        out_shape=(jax.ShapeDtypeStruct((n, d), F32), jax.ShapeDtypeStruct((n, d), h_dtype)),
        grid_spec=grid_spec,
        compiler_params=_params("arbitrary"),
        name="combine",
    )(pos1, pos2, x, meta, y_sorted, g_next.reshape(1, d))


def kernel(x_prompt, x_sample, cache_k, cache_v, state_conv_a, state_conv_c, page_table, norm_mix, w_in, conv_a_w,
           w_out_a, sb_bias, w_out_b, conv_c_w, conv_c_b, ln_c_g, ln_c_b, w_out_c, w_o, norm_ffn, w_router_group,
           b_router_group, w_router_expert, b_router_expert, w_gate, w_up, w_down, norm_final):
    bp, seq, d = x_prompt.shape
    db, ds, _ = x_sample.shape
    depth = w_in.shape[0]
    nh, hd = cache_k.shape[3], cache_k.shape[4]
    hw = nh * hd
    aw, cw = state_conv_a.shape[-1], state_conv_c.shape[-1]
    ng, ne = w_router_group.shape[-1], w_router_expert.shape[-1]
    kc = conv_c_w.shape[1]
    n_p, n_s = bp * seq, db * ds
    n = n_p + n_s
    c_q, c_k, c_c, c_g, c_end = 3 * aw, 3 * aw + hw, 3 * aw + 3 * hw, 3 * aw + 3 * hw + 2 * cw, w_in.shape[-1]
    tn = 512 if all(c % 512 == 0 for c in (aw, hw, cw, d)) else LANES
    tmx = _tile(math.gcd(n_p, n_s), 512, 16)
    tme = 512 if (TOP_K * n) % 512 == 0 else 64
    assert (TOP_K * n) % tme == 0
    n_tiles = TOP_K * n // tme
    tq = _tile(seq, 256, LANES)
    nq = seq // tq
    assert tq >= 2 * kc

    x = jnp.concatenate([x_prompt.reshape(n_p, d), x_sample.reshape(n_s, d)], axis=0)
    h = _rmsnorm(x, norm_mix[0], BF16)
    outs = {k: [] for k in ("kp", "vp", "ks", "vs", "ap", "as", "cp", "cs")}

    for l in range(depth):
        pa = _proj(h, w_in, l, 0, c_q, tn, (BF16,), name="proj_a")
        q = _proj(h, w_in, l, c_q, c_k, tn, (BF16,), name="proj_q")
        kv, kvb = _proj(h, w_in, l, c_k, c_c, tn, (F32, BF16), name="proj_kv")
        pc = _proj(h, w_in, l, c_c, c_g, tn, (BF16,), sigmoid_from=cw // tn, name="proj_c")
        gates = _proj(h, w_in, l, c_g, c_end, tn, (BF16,), sigmoid_from=0, name="proj_gates")

        conv_w = (conv_a_w[l], conv_c_w[l], conv_c_b[l], ln_c_g[l], ln_c_b[l])
        af_p, cf_p, ha_p, hc_p = _mix_prompt(pa, pc, bp, seq, aw, cw, *conv_w)
        pa_t = pa[n_p:].reshape(db, ds, 3 * aw).transpose(1, 0, 2)
        pc_t = pc[n_p:].reshape(db, ds, 2 * cw).transpose(1, 0, 2)
        af_s, cf_s, ha_s, hc_s = _mix_sample(pa_t, pc_t, state_conv_a[l].transpose(1, 0, 2),
                                             state_conv_c[l].transpose(1, 0, 2), aw, cw, *conv_w)
        af_s = af_s.transpose(1, 0, 2).reshape(n_s, aw)
        cf_s = cf_s.transpose(1, 0, 2).reshape(n_s, cw)

        at_p = _attn_prompt(q, kvb, sb_bias[l], bp, seq, nh, hd, tq, nq, 0, BF16, False, "attn_prompt")
        q_s = q[n_p:].astype(F32).reshape(db, ds, nh // 2, 2, hd)
        zq = jnp.zeros((db, ds, nh // 2, hd), F32)
        q_bd = jnp.concatenate([jnp.concatenate([q_s[:, :, :, 0], zq], axis=-1),
                                jnp.concatenate([zq, q_s[:, :, :, 1]], axis=-1)], axis=1)
        q_bd = q_bd.transpose(0, 2, 1, 3).reshape(db, (nh // 2) * 2 * ds, 2 * hd)
        kn = kv[n_p:, :hw].reshape(db, ds, hw)
        vn = kv[n_p:, hw:].reshape(db, ds, hw)
        raw = _attn_sample(q_bd, kn, vn, cache_k, cache_v, l, page_table, sb_bias[l], ds, nh, hd)
        raw = raw.reshape(db, nh // 2, 2, ds, 2, hd)
        at_s = jnp.stack([raw[:, :, 0, :, 0], raw[:, :, 1, :, 1]], axis=2)
        at_s = at_s.transpose(0, 3, 1, 2, 4).reshape(n_s, hw).astype(BF16)

        merged = _merge((af_p, at_p, cf_p), (af_s, at_s, cf_s), gates, w_out_a[l].astype(BF16),
                        w_out_b[l].astype(BF16), w_out_c[l].astype(BF16), tmx)
        wr = jnp.zeros((d, LANES), F32).at[:, :ng].set(w_router_group[l]).at[:, ng:ng + ne].set(w_router_expert[l])
        br = jnp.zeros((1, LANES), F32).at[0, :ng].set(b_router_group[l]).at[0, ng:ng + ne].set(b_router_expert[l])
        x_in = x
        x, h2, logits = _outproj(merged, x_in, w_o[l].astype(BF16), norm_ffn[l], wr, br, tmx)

        if l < depth - 1:
            x_t = x_in[:n_p].reshape(bp, seq, d)[:, seq - tq:].reshape(bp * tq, d)
            tp = lambda c0, c1, nm: _tail_proj(x_t, norm_mix[l], w_in, l, c0, c1, tn, nm)
            pa_x, q_x = tp(0, c_q, "tail_proj_a"), tp(c_q, c_k, "tail_proj_q")
            pc_x, gl_x = tp(c_c, c_g, "tail_proj_c"), tp(c_g, c_end, "tail_proj_gates")
            af_x, cf_x, _, _ = _mix_prompt(pa_x, pc_x, bp, tq, aw, cw, *conv_w, out_dtype=F32, gate_is_logit=True,
                                           name="tail_mix")
            at_x = _attn_prompt(q_x, kvb, sb_bias[l], bp, seq, nh, hd, tq, 1, nq - 1, F32, True, "tail_attn")
            merged_x = _tail_merge(af_x, at_x, cf_x, gl_x, w_out_a, w_out_b, w_out_c, l, tn)
            logits_x = _tail_router(merged_x, x_t, w_o, l, norm_ffn[l], wr, br, tn)
            keep = tq - 2 * kc
            logits = logits.reshape(-1, LANES)
            for b in range(bp):
                logits = lax.dynamic_update_slice(logits, logits_x[b * tq + keep:(b + 1) * tq],
                                                  (b * seq + seq - tq + keep, 0))

        meta, counts = _route(logits, ng, ne, tmx)
        e1, e2 = meta[:, 0].astype(jnp.int32), meta[:, 1].astype(jnp.int32)
        cnt = counts[0, ng:ng + ne].astype(jnp.int32)
        ends = jnp.cumsum(cnt)
        off = ends - cnt
        pos1 = off[e1] + meta[:, 4].astype(jnp.int32)
        pos2 = off[e2] + meta[:, 5].astype(jnp.int32)
        x_sorted = _dispatch(h2, pos1, pos2)
        marks = jnp.concatenate([jnp.arange(n_tiles, dtype=jnp.int32) * tme, off])
        ids = jnp.arange(marks.shape[0], dtype=jnp.int32)
        before = (marks[None, :] < marks[:, None]) | ((marks[None, :] == marks[:, None]) & (ids[None, :] < ids[:, None]))
        rank = jnp.sum(before, axis=1).astype(jnp.int32)
        starts = jnp.sum(jnp.where(rank[None, :] == ids[:, None], marks[None, :], 0), axis=1).astype(jnp.int32)
        stops = jnp.concatenate([starts[1:], jnp.full((1,), TOP_K * n, jnp.int32)])
        item_tile = jnp.minimum(starts // tme, n_tiles - 1)
        item_expert = jnp.minimum(jnp.sum(ends[None, :] <= starts[:, None], axis=1), ne - 1).astype(jnp.int32)
        y_sorted = _experts(x_sorted, item_tile, item_expert, starts - item_tile * tme, stops - item_tile * tme,
                            w_gate, w_up, w_down, l, tme)
        last = l == depth - 1
        x, h = _combine(x, meta, y_sorted, pos1, pos2, norm_final if last else norm_mix[l + 1], F32 if last else BF16)

        outs["kp"].append(kv[:n_p, :hw].reshape(bp, seq, nh, hd))
        outs["vp"].append(kv[:n_p, hw:].reshape(bp, seq, nh, hd))
        outs["ks"].append(kn.reshape(db, ds, nh, hd))
        outs["vs"].append(vn.reshape(db, ds, nh, hd))
        outs["ap"].append(ha_p)
        outs["as"].append(ha_s.transpose(1, 0, 2))
        outs["cp"].append(hc_p)
        outs["cs"].append(hc_s.transpose(1, 0, 2))

    return (h[:n_p].reshape(bp, seq, d), h[n_p:].reshape(db, ds, d),
            jnp.stack(outs["kp"]), jnp.stack(outs["vp"]), jnp.stack(outs["ks"]), jnp.stack(outs["vs"]),
            jnp.stack(outs["ap"]), jnp.stack(outs["as"]), jnp.stack(outs["cp"]), jnp.stack(outs["cs"]))
```

```python
import functools
import math

import jax
import jax.numpy as jnp
from jax import lax
from jax.experimental import pallas as pl
from jax.experimental.pallas import tpu as pltpu

BF16 = jnp.bfloat16
F32 = jnp.float32
RMS_EPS = 1e-6
LN_EPS = 1e-5
TOP_K = 2
LANES = 128
V7X_VMEM_BUDGET = 56 * 1024 * 1024
NEG = -3.0e38
LOG2E = 1.4426950408889634
NT_DIMS = (((1,), (1,)), ((), ()))


def _tile(n, target, align):
    if n <= target:
        return n
    t = (target // align) * align
    while t >= align:
        if n % t == 0:
            return t
        t -= align
    raise ValueError(f"no tile for {n} (target {target}, align {align})")


def _params(*sem):
    return pltpu.CompilerParams(dimension_semantics=sem or None, vmem_limit_bytes=V7X_VMEM_BUDGET)


def _sigmoid(x):
    return 0.5 * jnp.tanh(0.5 * x) + 0.5


def _split_bf16(x):
    hi = x.astype(BF16)
    lo = (x - hi.astype(F32)).astype(BF16)
    return hi, lo


def _dot(a, b):
    return jnp.dot(a, b, preferred_element_type=F32)


def _dot3(a_hi, a_lo, w):
    w_hi, w_lo = _split_bf16(w)
    return _dot(a_hi, w_hi) + _dot(a_hi, w_lo) + _dot(a_lo, w_hi)


def _rms(x, g):
    ms = jnp.mean(x * x, axis=-1, keepdims=True)
    return x * lax.rsqrt(ms + RMS_EPS) * g


def _two_part_specs(parts, tm):
    if len(parts) == 1:
        return [pl.BlockSpec((tm, parts[0].shape[1]), lambda i, *_: (i, 0))], None
    n0 = parts[0].shape[0] // tm
    return [pl.BlockSpec((tm, parts[0].shape[1]), lambda i, *_: (jnp.minimum(i, n0 - 1), 0)),
            pl.BlockSpec((tm, parts[1].shape[1]), lambda i, *_: (jnp.maximum(i - n0, 0), 0))], n0


def _pick_part(refs, n0, fn):
    if n0 is None:
        fn(refs[0])
        return
    i = pl.program_id(0)

    @pl.when(i < n0)
    def _():
        fn(refs[0])

    @pl.when(i >= n0)
    def _():
        fn(refs[1])


def _rmsnorm_kernel(*refs, n0):
    g_ref, o_ref = refs[-2:]

    def run(x_ref):
        o_ref[...] = _rms(x_ref[...], g_ref[...]).astype(o_ref.dtype)

    _pick_part(refs[:-2], n0, run)


def _rmsnorm(x_parts, g, out_dtype, tm):
    n, d = sum(x.shape[0] for x in x_parts), x_parts[0].shape[1]
    specs, n0 = _two_part_specs(x_parts, tm)
    return pl.pallas_call(
        functools.partial(_rmsnorm_kernel, n0=n0),
        out_shape=jax.ShapeDtypeStruct((n, d), out_dtype),
        grid=(n // tm,),
        in_specs=specs + [pl.BlockSpec((1, d), lambda i: (0, 0))],
        out_specs=pl.BlockSpec((tm, d), lambda i: (i, 0)),
        compiler_params=_params("arbitrary"),
        name="rmsnorm",
    )(*x_parts, g.reshape(1, d))


def _proj_kernel(h_ref, w_ref, *o_refs, sigmoid_from):
    acc = _dot(h_ref[...], w_ref[...].astype(BF16))

    def store(val):
        for o_ref in o_refs:
            o_ref[...] = val.astype(o_ref.dtype)

    if sigmoid_from is None:
        store(acc)
    else:
        j = pl.program_id(1)

        @pl.when(j >= sigmoid_from)
        def _():
            store(_sigmoid(acc))

        @pl.when(j < sigmoid_from)
        def _():
            store(acc)


def _proj(h, w_in, layer, c0, c1, tn, out_dtypes, sigmoid_from=None, name="proj"):
    n, k = h.shape
    tm = _tile(n, 2176, 16)
    cb0 = c0 // tn
    outs = pl.pallas_call(
        functools.partial(_proj_kernel, sigmoid_from=sigmoid_from),
        out_shape=tuple(jax.ShapeDtypeStruct((n, c1 - c0), dt) for dt in out_dtypes),
        grid=(n // tm, (c1 - c0) // tn),
        in_specs=[pl.BlockSpec((tm, k), lambda i, j: (i, 0)),
                  pl.BlockSpec((None, k, tn), lambda i, j: (layer, 0, cb0 + j))],
        out_specs=tuple(pl.BlockSpec((tm, tn), lambda i, j: (i, j)) for _ in out_dtypes),
        compiler_params=_params("arbitrary", "arbitrary"),
        name=name,
    )(h, w_in)
    return outs if len(out_dtypes) > 1 else outs[0]


def _tail_proj_kernel(x_ref, g_ref, w_ref, o_ref, hh_ref, hl_ref):
    @pl.when(pl.program_id(0) == 0)
    def _():
        hi, lo = _split_bf16(_rms(x_ref[...], g_ref[...]))
        hh_ref[...] = hi
        hl_ref[...] = lo

    o_ref[...] = _dot3(hh_ref[...], hl_ref[...], w_ref[...])


def _tail_proj(x_t, g, w_in, layer, c0, c1, tn, name):
    r, d = x_t.shape
    cb0 = c0 // tn
    return pl.pallas_call(
        _tail_proj_kernel,
        out_shape=jax.ShapeDtypeStruct((r, c1 - c0), F32),
        grid=((c1 - c0) // tn,),
        in_specs=[pl.BlockSpec((r, d), lambda j: (0, 0)), pl.BlockSpec((1, d), lambda j: (0, 0)),
                  pl.BlockSpec((None, d, tn), lambda j: (layer, 0, cb0 + j))],
        out_specs=pl.BlockSpec((r, tn), lambda j: (0, j)),
        scratch_shapes=[pltpu.VMEM((r, d), BF16), pltpu.VMEM((r, d), BF16)],
        compiler_params=_params("arbitrary"),
        name=name,
    )(x_t, g.reshape(1, d), w_in)


def _ln_silu(acc, g, b):
    mu = jnp.mean(acc, axis=-1, keepdims=True)
    xc = acc - mu
    var = jnp.mean(xc * xc, axis=-1, keepdims=True)
    y = xc * lax.rsqrt(var + LN_EPS) * g + b
    return y * _sigmoid(y)


def _mix_prompt_kernel(ab_ref, ac_ref, ax_ref, cv_ref, cg_ref, caw_ref, ccw_ref, ccb_ref, lng_ref, lnb_ref,
                       af_ref, cf_ref, ha_ref, hc_ref, exa_ref, exc_ref, *, tt, ka, kc, pa, pc, gate_is_logit):
    ti = pl.program_id(1)

    @pl.when(ti == 0)
    def _():
        exa_ref[0:pa, :] = jnp.zeros((pa, exa_ref.shape[1]), F32)
        exc_ref[0:pc, :] = jnp.zeros((pc, exc_ref.shape[1]), F32)

    u = ac_ref[...].astype(F32) * ax_ref[...].astype(F32)
    exa_ref[pa:pa + tt, :] = u
    conv = caw_ref[ka - 1:ka, :] * u
    for j in range(ka - 1):
        conv = conv + caw_ref[j:j + 1, :] * exa_ref[pa - (ka - 1) + j:pa - (ka - 1) + j + tt, :]
    af_ref[...] = (ab_ref[...].astype(F32) * conv).astype(af_ref.dtype)

    cg = cg_ref[...].astype(F32)
    glu = cv_ref[...].astype(F32) * (_sigmoid(cg) if gate_is_logit else cg)
    exc_ref[pc:pc + tt, :] = glu
    acc = ccb_ref[...] + ccw_ref[kc - 1:kc, :] * glu
    for j in range(kc - 1):
        acc = acc + ccw_ref[j:j + 1, :] * exc_ref[pc - (kc - 1) + j:pc - (kc - 1) + j + tt, :]
    cf_ref[...] = _ln_silu(acc, lng_ref[...], lnb_ref[...]).astype(cf_ref.dtype)

    @pl.when(ti == pl.num_programs(1) - 1)
    def _():
        ha_ref[...] = exa_ref[pa + tt - (ka - 1):pa + tt, :]
        hc_ref[...] = exc_ref[pc + tt - (kc - 1):pc + tt, :]

    exa_ref[0:pa, :] = exa_ref[tt:tt + pa, :]
    exc_ref[0:pc, :] = exc_ref[tt:tt + pc, :]


def _mix_prompt(pa_arr, pc_arr, bp, seq, aw, cw, caw, ccw, ccb, lng, lnb, out_dtype=None, gate_is_logit=False,
                name="mix_prompt"):
    out_dtype = out_dtype or BF16
    ka, kc = caw.shape[0], ccw.shape[0]
    tt = _tile(seq, 512, 32)
    nt = seq // tt
    pa, pc = 8, 32
    assert ka - 1 <= pa and kc - 1 <= pc and tt >= pc
    row = lambda c: (lambda b, t: (b * nt + t, c))
    const = lambda b, t: (0, 0)
    return pl.pallas_call(
        functools.partial(_mix_prompt_kernel, tt=tt, ka=ka, kc=kc, pa=pa, pc=pc, gate_is_logit=gate_is_logit),
        out_shape=(jax.ShapeDtypeStruct((bp * seq, aw), out_dtype), jax.ShapeDtypeStruct((bp * seq, cw), out_dtype),
                   jax.ShapeDtypeStruct((bp, ka - 1, aw), F32), jax.ShapeDtypeStruct((bp, kc - 1, cw), F32)),
        grid=(bp, nt),
        in_specs=[pl.BlockSpec((tt, aw), row(0)), pl.BlockSpec((tt, aw), row(1)), pl.BlockSpec((tt, aw), row(2)),
                  pl.BlockSpec((tt, cw), row(0)), pl.BlockSpec((tt, cw), row(1)),
                  pl.BlockSpec((ka, aw), const), pl.BlockSpec((kc, cw), const), pl.BlockSpec((1, cw), const),
                  pl.BlockSpec((1, cw), const), pl.BlockSpec((1, cw), const)],
        out_specs=(pl.BlockSpec((tt, aw), row(0)), pl.BlockSpec((tt, cw), row(0)),
                   pl.BlockSpec((None, ka - 1, aw), lambda b, t: (b, 0, 0)),
                   pl.BlockSpec((None, kc - 1, cw), lambda b, t: (b, 0, 0))),
        scratch_shapes=[pltpu.VMEM((pa + tt, aw), F32), pltpu.VMEM((pc + tt, cw), F32)],
        compiler_params=_params("arbitrary", "arbitrary"),
        name=name,
    )(pa_arr, pa_arr, pa_arr, pc_arr, pc_arr, caw, ccw, ccb.reshape(1, cw), lng.reshape(1, cw), lnb.reshape(1, cw))


def _mix_sample_kernel(pa_ref, pc_ref, ha_ref, hc_ref, caw_ref, ccw_ref, ccb_ref, lng_ref, lnb_ref,
                       af_ref, cf_ref, nha_ref, nhc_ref, *, aw, cw):
    ds = pa_ref.shape[0]
    ka, kc = caw_ref.shape[0], ccw_ref.shape[0]
    u = [pa_ref[t, :, aw:2 * aw].astype(F32) * pa_ref[t, :, 2 * aw:3 * aw].astype(F32) for t in range(ds)]
    glu = [pc_ref[t, :, 0:cw].astype(F32) * pc_ref[t, :, cw:2 * cw].astype(F32) for t in range(ds)]
    full_a = lambda i: ha_ref[i] if i < ka - 1 else u[i - (ka - 1)]
    full_c = lambda i: hc_ref[i] if i < kc - 1 else glu[i - (kc - 1)]
    for t in range(ds):
        conv = caw_ref[0:1, :] * full_a(t)
        for j in range(1, ka):
            conv = conv + caw_ref[j:j + 1, :] * full_a(t + j)
        af_ref[t] = (pa_ref[t, :, 0:aw].astype(F32) * conv).astype(af_ref.dtype)
        acc = ccb_ref[...] + ccw_ref[0:1, :] * full_c(t)
        for j in range(1, kc):
            acc = acc + ccw_ref[j:j + 1, :] * full_c(t + j)
        cf_ref[t] = _ln_silu(acc, lng_ref[...], lnb_ref[...]).astype(cf_ref.dtype)
    for i in range(ka - 1):
        nha_ref[i] = full_a(i + ds)
    for i in range(kc - 1):
        nhc_ref[i] = full_c(i + ds)


def _mix_sample(pa_t, pc_t, ha_t, hc_t, aw, cw, caw, ccw, ccb, lng, lnb):
    ds, db, _ = pa_t.shape
    ka, kc = caw.shape[0], ccw.shape[0]
    return pl.pallas_call(
        functools.partial(_mix_sample_kernel, aw=aw, cw=cw),
        out_shape=(jax.ShapeDtypeStruct((ds, db, aw), BF16), jax.ShapeDtypeStruct((ds, db, cw), BF16),
                   jax.ShapeDtypeStruct((ka - 1, db, aw), F32), jax.ShapeDtypeStruct((kc - 1, db, cw), F32)),
        compiler_params=_params(),
        name="mix_sample",
    )(pa_t, pc_t, ha_t, hc_t, caw, ccw, ccb.reshape(1, cw), lng.reshape(1, cw), lnb.reshape(1, cw))


def _suffix_matrix(bk):
    j = lax.broadcasted_iota(jnp.int32, (bk, bk), 0)
    s = lax.broadcasted_iota(jnp.int32, (bk, bk), 1)
    return (j >= s).astype(BF16)


def _sb_block(z2, mask, tmat, carry, two_pass):
    bk = z2.shape[1]
    sp = jnp.maximum(z2, 0.0) + jnp.log2(1.0 + jnp.exp2(-jnp.abs(z2)))
    if mask is not None:
        sp = jnp.where(mask, sp, 0.0)
    if two_pass:
        hi, lo = _split_bf16(sp)
        suf = _dot(hi, tmat) + _dot(lo, tmat)
    else:
        suf = _dot(sp.astype(BF16), tmat)
    c = suf + jnp.concatenate([carry] * (bk // LANES), axis=1)
    a = jnp.exp2(z2 - c)
    if mask is not None:
        a = jnp.where(mask, a, 0.0)
    return a, carry + jnp.broadcast_to(suf[:, 0:1], carry.shape)


def _attn_prompt_kernel(bias_ref, q_ref, k_ref, v_ref, tmat_ref, o_ref, acc_ref, carry_ref,
                        *, tq, bk, hd, hg, qi_base, scale, precise):
    g = pl.program_id(1)
    q0 = (qi_base + pl.program_id(2)) * tq
    tmat = tmat_ref[...]
    lanes = lambda h: slice(h * hd, (h + 1) * hd)

    qs = []
    for h in range(hg):
        qh = q_ref[:, lanes(h)]
        qs.append(_split_bf16(qh) if precise else (qh.astype(BF16),))

    def key_block(kb, mask, first):
        s0 = pl.multiple_of(kb * bk, bk)
        z2, sp = [], []
        for h in range(hg):
            kblk = k_ref[pl.ds(s0, bk), lanes(h)]
            s = sum(lax.dot_general(part, kblk, NT_DIMS, preferred_element_type=F32) for part in qs[h])
            z = s * (scale * LOG2E) + bias_ref[g * hg + h] * LOG2E
            soft = jnp.maximum(z, 0.0) + jnp.log2(1.0 + jnp.exp2(-jnp.abs(z)))
            z2.append(z)
            sp.append(soft if mask is None else jnp.where(mask, soft, 0.0))
        sp = jnp.concatenate(sp, axis=0)
        if precise:
            hi, lo = _split_bf16(sp)
            suf_all = _dot(hi, tmat) + _dot(lo, tmat)
        else:
            suf_all = _dot(sp.astype(BF16), tmat)
        for h in range(hg):
            suf = suf_all[h * tq:(h + 1) * tq]
            carry = jnp.zeros((tq, LANES), F32) if first else carry_ref[h]
            a = jnp.exp2(z2[h] - (suf + jnp.concatenate([carry] * (bk // LANES), axis=1)))
            if mask is not None:
                a = jnp.where(mask, a, 0.0)
            pv = _dot(a.astype(BF16), v_ref[pl.ds(s0, bk), lanes(h)])
            if first:
                acc_ref[:, lanes(h)] = pv
            else:
                acc_ref[:, lanes(h)] += pv
            carry_ref[h] = carry + jnp.broadcast_to(suf[:, 0:1], carry.shape)

    nfull = q0 // bk
    r = lax.broadcasted_iota(jnp.int32, (tq, bk), 0)
    c = lax.broadcasted_iota(jnp.int32, (tq, bk), 1)
    ndiag = tq // bk
    for d in range(ndiag - 1, -1, -1):
        key_block(nfull + d, c + d * bk < r, d == ndiag - 1)

    def body(it, _):
        key_block(nfull - 1 - 2 * it, None, False)
        key_block(nfull - 2 - 2 * it, None, False)
        return 0

    lax.fori_loop(0, nfull // 2, body, 0)

    @pl.when(nfull % 2 == 1)
    def _():
        key_block(0, None, False)

    o_ref[...] = acc_ref[...].astype(o_ref.dtype)


def _attn_prompt(q, kvb, bias, bp, seq, nh, hd, tq, bk, nq, qi_base, out_dtype, precise, name):
    hg = 4 if nh % 4 == 0 else 2
    ng = nh // hg
    assert tq % bk == 0 and seq % tq == 0
    return pl.pallas_call(
        functools.partial(_attn_prompt_kernel, tq=tq, bk=bk, hd=hd, hg=hg, qi_base=qi_base, scale=hd ** -0.5,
                          precise=precise),
        out_shape=jax.ShapeDtypeStruct((bp * nq * tq, nh * hd), out_dtype),
        grid=(bp, ng, nq),
        in_specs=[pl.BlockSpec(memory_space=pltpu.SMEM),
                  pl.BlockSpec((tq, hg * hd), lambda b, g, i: (b * nq + i, g)),
                  pl.BlockSpec((seq, hg * hd), lambda b, g, i: (b, g)),
                  pl.BlockSpec((seq, hg * hd), lambda b, g, i: (b, ng + g)),
                  pl.BlockSpec((bk, bk), lambda b, g, i: (0, 0))],
        out_specs=pl.BlockSpec((tq, hg * hd), lambda b, g, i: (b * nq + i, g)),
        scratch_shapes=[pltpu.VMEM((tq, hg * hd), F32), pltpu.VMEM((hg, tq, LANES), F32)],
        compiler_params=_params("arbitrary", "arbitrary", "arbitrary"),
        name=name,
    )(bias, q, kvb, kvb, _suffix_matrix(bk))


def _attn_sample_kernel(pt_ref, bias_ref, q_ref, kn_ref, vn_ref, tmat_ref, ck_hbm, cv_hbm, o_ref,
                        knf_ref, vnf_ref, kbuf, vbuf, sem, *, layer, npg, ds, hd, nh, page, scale):
    n = pl.program_id(0)
    rr, ww = 2 * ds, 2 * hd
    p2 = nh // 2
    rows = p2 * rr

    def page_copies(seq_idx, slot):
        for pg in range(npg):
            src = pt_ref[seq_idx * npg + pg]
            yield pltpu.make_async_copy(ck_hbm.at[layer, src], kbuf.at[slot, pg], sem.at[slot])
            yield pltpu.make_async_copy(cv_hbm.at[layer, src], vbuf.at[slot, pg], sem.at[slot])

    @pl.when(n == 0)
    def _():
        for cp in page_copies(0, 0):
            cp.start()

    @pl.when(n + 1 < pl.num_programs(0))
    def _():
        for cp in page_copies(n + 1, (n + 1) % 2):
            cp.start()

    slot = n % 2
    for cp in page_copies(n, slot):
        cp.wait()

    knf_ref[...] = jnp.zeros(knf_ref.shape, F32)
    vnf_ref[...] = jnp.zeros(vnf_ref.shape, F32)
    knf_ref[0:ds, :] = kn_ref[...]
    vnf_ref[0:ds, :] = vn_ref[...]

    row = lax.broadcasted_iota(jnp.int32, (rows, LANES), 0)
    col = lax.broadcasted_iota(jnp.int32, (rows, LANES), 1)
    head = (row // rr) * 2 + (row % rr) // ds
    bias = jnp.zeros((rows, LANES), F32)
    for hh in range(nh):
        bias = jnp.where(head == hh, bias_ref[hh] * LOG2E, bias)
    new_mask = col < (row % ds)

    lhs = [q_ref[p * rr:(p + 1) * rr, :].astype(BF16) for p in range(p2)]

    def new_tile(ref, p):
        return ref[:, p * ww:(p + 1) * ww].astype(BF16)

    def page_tile(buf, pages, p):
        per_page = [jnp.concatenate([buf.at[slot, pg][pl.ds(2 * p + e, page, stride=nh), :] for e in range(2)], axis=1)
                    for pg in pages]
        return jnp.concatenate(per_page, axis=0).astype(BF16)

    def block(k_tile, v_tile, nkeys, mask, carry, acc):
        s = jnp.concatenate(
            [lax.dot_general(lhs[p], k_tile(p), NT_DIMS, preferred_element_type=F32) for p in range(p2)], axis=0)
        z2 = s * (scale * LOG2E) + jnp.concatenate([bias] * (nkeys // LANES), axis=1)
        a, carry = _sb_block(z2, mask, tmat_ref[0:nkeys, 0:nkeys], carry, True)
        acc = [acc[p] + _dot(a[p * rr:(p + 1) * rr, :].astype(BF16), v_tile(p)) for p in range(p2)]
        return carry, acc

    carry = jnp.zeros((rows, LANES), F32)
    acc = [jnp.zeros((rr, ww), F32) for _ in range(p2)]
    carry, acc = block(functools.partial(new_tile, knf_ref), functools.partial(new_tile, vnf_ref), page, new_mask,
                       carry, acc)
    hi_pg = npg
    while hi_pg > 0:
        pages = list(range(max(hi_pg - 2, 0), hi_pg))
        carry, acc = block(functools.partial(page_tile, kbuf, pages), functools.partial(page_tile, vbuf, pages),
                           len(pages) * page, None, carry, acc)
        hi_pg -= len(pages)
    for p in range(p2):
        o_ref[p * rr:(p + 1) * rr, :] = acc[p]


def _attn_sample(q_bd, kn, vn, cache_k, cache_v, layer, page_table, bias, ds, nh, hd):
    db = q_bd.shape[0]
    npg = page_table.shape[1]
    depth, npool, page = cache_k.shape[:3]
    hw = nh * hd
    assert page == LANES and hd == LANES and (2 * ds) % 8 == 0 and nh % 2 == 0
    ck = cache_k.reshape(depth, npool, page * nh, hd)
    cv = cache_v.reshape(depth, npool, page * nh, hd)
    rows = (nh // 2) * 2 * ds
    seq_blk = lambda shape: pl.BlockSpec((None,) + shape, lambda n, pt: (n, 0, 0))
    grid_spec = pltpu.PrefetchScalarGridSpec(
        num_scalar_prefetch=1,
        grid=(db,),
        in_specs=[pl.BlockSpec(memory_space=pltpu.SMEM), seq_blk((rows, 2 * hd)), seq_blk((ds, hw)), seq_blk((ds, hw)),
                  pl.BlockSpec((2 * page, 2 * page), lambda n, pt: (0, 0)),
                  pl.BlockSpec(memory_space=pl.ANY), pl.BlockSpec(memory_space=pl.ANY)],
        out_specs=seq_blk((rows, 2 * hd)),
        scratch_shapes=[pltpu.VMEM((page, hw), F32), pltpu.VMEM((page, hw), F32),
                        pltpu.VMEM((2, npg, page * nh, hd), F32), pltpu.VMEM((2, npg, page * nh, hd), F32),
                        pltpu.SemaphoreType.DMA((2,))],
    )
    return pl.pallas_call(
        functools.partial(_attn_sample_kernel, layer=layer, npg=npg, ds=ds, hd=hd, nh=nh, page=page,
                          scale=hd ** -0.5),
        out_shape=jax.ShapeDtypeStruct((db, rows, 2 * hd), F32),
        grid_spec=grid_spec,
        compiler_params=_params("arbitrary"),
        name="attn_sample",
    )(page_table.reshape(-1), bias, q_bd, kn, vn, _suffix_matrix(2 * page), ck, cv)


def _merge_kernel(afp_ref, atp_ref, cfp_ref, afs_ref, ats_ref, cfs_ref, g0_ref, g1_ref, g2_ref,
                  wa_ref, wb_ref, wc_ref, o_ref, *, n_prompt_tiles):
    def run(af_ref, at_ref, cf_ref):
        ya = _dot(af_ref[...], wa_ref[...])
        yb = _dot(at_ref[...], wb_ref[...])
        yc = _dot(cf_ref[...], wc_ref[...])
        o_ref[...] = (g0_ref[...].astype(F32) * ya + g1_ref[...].astype(F32) * yb
                      + g2_ref[...].astype(F32) * yc).astype(o_ref.dtype)

    i = pl.program_id(0)

    @pl.when(i < n_prompt_tiles)
    def _():
        run(afp_ref, atp_ref, cfp_ref)

    @pl.when(i >= n_prompt_tiles)
    def _():
        run(afs_ref, ats_ref, cfs_ref)


def _merge(feats_p, feats_s, gates, wa, wb, wc, tm):
    n_p, n_s = feats_p[0].shape[0], feats_s[0].shape[0]
    d = wa.shape[1]
    npt, nst = n_p // tm, n_s // tm
    prow = lambda a: pl.BlockSpec((tm, a.shape[1]), lambda i: (jnp.minimum(i, npt - 1), 0))
    srow = lambda a: pl.BlockSpec((tm, a.shape[1]), lambda i: (jnp.maximum(i - npt, 0), 0))
    gate = lambda k: pl.BlockSpec((tm, d), lambda i: (i, k))
    resident = lambda w: pl.BlockSpec(w.shape, lambda i: (0, 0), pipeline_mode=pl.Buffered(1))
    return pl.pallas_call(
        functools.partial(_merge_kernel, n_prompt_tiles=npt),
        out_shape=jax.ShapeDtypeStruct((n_p + n_s, d), BF16),
        grid=(npt + nst,),
        in_specs=[prow(a) for a in feats_p] + [srow(a) for a in feats_s] + [gate(0), gate(1), gate(2),
                  resident(wa), resident(wb), resident(wc)],
        out_specs=pl.BlockSpec((tm, d), lambda i: (i, 0)),
        compiler_params=_params("arbitrary"),
        name="merge",
    )(*feats_p, *feats_s, gates, gates, gates, wa, wb, wc)


def _router_logits(h, wr, br):
    hh, hl = _split_bf16(h)
    return _dot3(hh, hl, wr) + br


def _outproj_kernel(m_ref, *refs, n0):
    wo_ref, g_ref, wr_ref, br_ref, xo_ref, h_ref, lg_ref = refs[-7:]

    def run(x_ref):
        xn = x_ref[...] + _dot(m_ref[...], wo_ref[...])
        xo_ref[...] = xn
        h = _rms(xn, g_ref[...])
        h_ref[...] = h
        lg_ref[...] = _router_logits(h, wr_ref[...], br_ref[...])

    _pick_part(refs[:-7], n0, run)


def _outproj(merged, x_parts, wo, g, wr, br, tm):
    n, d = merged.shape
    rowb = lambda w: pl.BlockSpec((tm, w), lambda i: (i, 0))
    const = lambda shape: pl.BlockSpec(shape, lambda i: (0, 0))
    x_specs, n0 = _two_part_specs(x_parts, tm)
    return pl.pallas_call(
        functools.partial(_outproj_kernel, n0=n0),
        out_shape=(jax.ShapeDtypeStruct((n, d), F32), jax.ShapeDtypeStruct((n, d), F32),
                   jax.ShapeDtypeStruct((n, LANES), F32)),
        grid=(n // tm,),
        in_specs=[rowb(d)] + x_specs + [pl.BlockSpec((d, d), lambda i: (0, 0), pipeline_mode=pl.Buffered(1)),
                                        const((1, d)), const((d, LANES)), const((1, LANES))],
        out_specs=(rowb(d), rowb(d), rowb(LANES)),
        compiler_params=_params("arbitrary"),
        name="outproj",
    )(merged, *x_parts, wo, g.reshape(1, d), wr, br)


def _tail_merge_kernel(af_ref, at_ref, cf_ref, g0_ref, g1_ref, g2_ref, wa_ref, wb_ref, wc_ref, o_ref):
    y = [_dot3(*_split_bf16(f_ref[...]), w_ref[...])
         for f_ref, w_ref in ((af_ref, wa_ref), (at_ref, wb_ref), (cf_ref, wc_ref))]
    o_ref[...] = (_sigmoid(g0_ref[...]) * y[0] + _sigmoid(g1_ref[...]) * y[1]
                  + _sigmoid(g2_ref[...]) * y[2])


def _tail_merge(af, at, cf, gate_logits, wa, wb, wc, layer, tn):
    r, d = af.shape[0], wa.shape[-1]
    full = lambda a: pl.BlockSpec(a.shape, lambda j: (0, 0))
    gate = lambda k: pl.BlockSpec((r, tn), lambda j: (0, k * (d // tn) + j))
    wcol = lambda w: pl.BlockSpec((None, w.shape[1], tn), lambda j: (layer, 0, j))
    return pl.pallas_call(
        _tail_merge_kernel,
        out_shape=jax.ShapeDtypeStruct((r, d), F32),
        grid=(d // tn,),
        in_specs=[full(af), full(at), full(cf), gate(0), gate(1), gate(2), wcol(wa), wcol(wb), wcol(wc)],
        out_specs=pl.BlockSpec((r, tn), lambda j: (0, j)),
        compiler_params=_params("arbitrary"),
        name="tail_merge",
    )(af, at, cf, gate_logits, gate_logits, gate_logits, wa, wb, wc)


def _tail_router_kernel(m_ref, x_ref, wo_ref, g_ref, wr_ref, br_ref, lg_ref, mh_ref, ml_ref, xn_ref, *, tn):
    j = pl.program_id(0)

    @pl.when(j == 0)
    def _():
        hi, lo = _split_bf16(m_ref[...])
        mh_ref[...] = hi
        ml_ref[...] = lo

    col = pl.multiple_of(j * tn, tn)
    xn_ref[:, pl.ds(col, tn)] = x_ref[...] + _dot3(mh_ref[...], ml_ref[...], wo_ref[...])

    @pl.when(j == pl.num_programs(0) - 1)
    def _():
        lg_ref[...] = _router_logits(_rms(xn_ref[...], g_ref[...]), wr_ref[...], br_ref[...])


def _tail_router(merged, x_t, wo, layer, g, wr, br, tn):
    r, d = x_t.shape
    const = lambda shape: pl.BlockSpec(shape, lambda j: (0, 0))
    return pl.pallas_call(
        functools.partial(_tail_router_kernel, tn=tn),
        out_shape=jax.ShapeDtypeStruct((r, LANES), F32),
        grid=(d // tn,),
        in_specs=[const((r, d)), pl.BlockSpec((r, tn), lambda j: (0, j)),
                  pl.BlockSpec((None, d, tn), lambda j: (layer, 0, j)), const((1, d)), const((d, LANES)),
                  const((1, LANES))],
        out_specs=const((r, LANES)),
        scratch_shapes=[pltpu.VMEM((r, d), BF16), pltpu.VMEM((r, d), BF16), pltpu.VMEM((r, d), F32)],
        compiler_params=_params("arbitrary"),
        name="tail_router",
    )(merged, x_t, wo, g.reshape(1, d), wr, br)


def _route_kernel(lg_ref, tri_ref, meta_ref, cnt_ref, run_ref, *, ng, ne):
    i = pl.program_id(0)

    @pl.when(i == 0)
    def _():
        run_ref[...] = jnp.zeros(run_ref.shape, F32)

    x = lg_ref[...]
    epg = ne // ng
    lane = lax.broadcasted_iota(jnp.int32, x.shape, 1).astype(F32)
    rmax = lambda v: jnp.max(v, axis=1, keepdims=True)
    rmin = lambda v: jnp.min(v, axis=1, keepdims=True)
    rsum = lambda v: jnp.sum(v, axis=1, keepdims=True)

    is_g = lane < ng
    gl = jnp.where(is_g, x, NEG)
    gmax = rmax(gl)
    gidx = rmin(jnp.where(is_g & (gl == gmax), lane, float(LANES)))
    gw = 1.0 / rsum(jnp.where(is_g, jnp.exp(jnp.minimum(x - gmax, 0.0)), 0.0))

    lo = ng + gidx * epg
    in_grp = (lane >= lo) & (lane < lo + epg)
    el = jnp.where(in_grp, x, NEG)
    v1 = rmax(el)
    i1 = rmin(jnp.where(in_grp & (el == v1), lane, float(LANES)))
    rest = in_grp & (lane != i1)
    el2 = jnp.where(rest, x, NEG)
    v2 = rmax(el2)
    i2 = rmin(jnp.where(rest & (el2 == v2), lane, float(LANES)))
    t = jnp.exp(v2 - v1)
    w1 = gw / (1.0 + t)
    w2 = gw * t / (1.0 + t)

    sel1, sel2 = lane == i1, lane == i2
    onehot = (sel1 | sel2).astype(BF16)
    rank = _dot(tri_ref[...], onehot) + run_ref[...]
    r1 = rsum(jnp.where(sel1, rank, 0.0))
    r2 = rsum(jnp.where(sel2, rank, 0.0))
    run_ref[...] += jnp.sum(onehot.astype(F32), axis=0, keepdims=True)
    cnt_ref[...] = run_ref[...]

    vals = [i1 - ng, i2 - ng, w1, w2, r1, r2]
    meta = jnp.zeros(x.shape, F32)
    for k, v in enumerate(vals):
        meta = jnp.where(lane == k, v, meta)
    meta_ref[...] = meta


def _route(logits, ng, ne, tr):
    n = logits.shape[0]
    r = lax.broadcasted_iota(jnp.int32, (tr, tr), 0)
    c = lax.broadcasted_iota(jnp.int32, (tr, tr), 1)
    tri = (c < r).astype(BF16)
    return pl.pallas_call(
        functools.partial(_route_kernel, ng=ng, ne=ne),
        out_shape=(jax.ShapeDtypeStruct((n, LANES), F32), jax.ShapeDtypeStruct((1, LANES), F32)),
        grid=(n // tr,),
        in_specs=[pl.BlockSpec((tr, LANES), lambda i: (i, 0)), pl.BlockSpec((tr, tr), lambda i: (0, 0))],
        out_specs=(pl.BlockSpec((tr, LANES), lambda i: (i, 0)), pl.BlockSpec((1, LANES), lambda i: (0, 0))),
        scratch_shapes=[pltpu.VMEM((1, LANES), F32)],
        compiler_params=_params("arbitrary"),
        name="route",
    )(logits, tri)


def _dispatch_kernel(p1_ref, p2_ref, h_ref, xs_hbm, sem, *, tc):
    i = pl.program_id(0)

    def row(r, carry):
        t = i * tc + r
        pltpu.make_async_copy(h_ref.at[pl.ds(r, 1)], xs_hbm.at[pl.ds(p1_ref[t], 1)], sem).start()
        pltpu.make_async_copy(h_ref.at[pl.ds(r, 1)], xs_hbm.at[pl.ds(p2_ref[t], 1)], sem).start()
        return carry

    lax.fori_loop(0, tc, row, 0, unroll=8)
    for _ in range(TOP_K):
        pltpu.make_async_copy(h_ref, xs_hbm.at[pl.ds(0, tc)], sem).wait()


def _dispatch(h2, pos1, pos2):
    n, d = h2.shape
    tc = _tile(n, 256, 8)
    grid_spec = pltpu.PrefetchScalarGridSpec(
        num_scalar_prefetch=2,
        grid=(n // tc,),
        in_specs=[pl.BlockSpec((tc, d), lambda i, a, b: (i, 0))],
        out_specs=pl.BlockSpec(memory_space=pl.ANY),
        scratch_shapes=[pltpu.SemaphoreType.DMA(())],
    )
    return pl.pallas_call(
        functools.partial(_dispatch_kernel, tc=tc),
        out_shape=jax.ShapeDtypeStruct((TOP_K * n, d), h2.dtype),
        grid_spec=grid_spec,
        compiler_params=_params("arbitrary"),
        name="dispatch",
    )(pos1, pos2, h2)


def _expert_kernel(it_ref, ie_ref, lo_ref, hi_ref, x_ref, wg_ref, wu_ref, wd_ref, y_ref, wgb, wub, wdb, cur_ref):
    g = pl.program_id(0)
    first_of_tile = (g == 0) | (it_ref[g] != it_ref[jnp.maximum(g - 1, 0)])
    lo, hi = lo_ref[g], hi_ref[g]

    @pl.when(g == 0)
    def _():
        cur_ref[0] = -1

    @pl.when(hi > lo)
    def _():
        @pl.when(cur_ref[0] != ie_ref[g])
        def _():
            wgb[...] = wg_ref[...].astype(BF16)
            wub[...] = wu_ref[...].astype(BF16)
            wdb[...] = wd_ref[...].astype(BF16)
            cur_ref[0] = ie_ref[g]

        x = x_ref[...].astype(BF16)
        gate = _dot(x, wgb[...])
        act = gate * _sigmoid(gate) * _dot(x, wub[...])
        y = _dot(act.astype(BF16), wdb[...])
        row = lax.broadcasted_iota(jnp.int32, y.shape, 0)
        y = jnp.where((row >= lo) & (row < hi), y, 0.0)

        @pl.when(first_of_tile)
        def _():
            y_ref[...] = y

        @pl.when(jnp.logical_not(first_of_tile))
        def _():
            y_ref[...] += y

    @pl.when((hi <= lo) & first_of_tile)
    def _():
        y_ref[...] = jnp.zeros(y_ref.shape, F32)


def _experts(x_sorted, item_tile, item_expert, item_lo, item_hi, w_gate, w_up, w_down, layer, tm):
    rows, d = x_sorted.shape
    f = w_gate.shape[-1]
    wspec = lambda a, b: pl.BlockSpec((None, None, a, b), lambda g, it, ie, lo, hi: (layer, ie[g], 0, 0))
    grid_spec = pltpu.PrefetchScalarGridSpec(
        num_scalar_prefetch=4,
        grid=(item_tile.shape[0],),
        in_specs=[pl.BlockSpec((tm, d), lambda g, it, ie, lo, hi: (it[g], 0)), wspec(d, f), wspec(d, f), wspec(f, d)],
        out_specs=pl.BlockSpec((tm, d), lambda g, it, ie, lo, hi: (it[g], 0)),
        scratch_shapes=[pltpu.VMEM((d, f), BF16), pltpu.VMEM((d, f), BF16), pltpu.VMEM((f, d), BF16),
                        pltpu.SMEM((1,), jnp.int32)],
    )
    return pl.pallas_call(
        _expert_kernel,
        out_shape=jax.ShapeDtypeStruct((rows, d), F32),
        grid_spec=grid_spec,
        compiler_params=_params("arbitrary"),
        name="experts",
    )(item_tile, item_expert, item_lo, item_hi, x_sorted, w_gate, w_up, w_down)


def _combine_kernel(p1_ref, p2_ref, x_ref, meta_ref, y_hbm, g_ref, o0_ref, o1_ref, ybuf, sem, *, tc, n0):
    i = pl.program_id(0)
    nsteps = pl.num_programs(0)

    def start_gather(tile, slot):
        def body(r, carry):
            t = tile * tc + r
            pltpu.make_async_copy(y_hbm.at[pl.ds(p1_ref[t], 1)], ybuf.at[slot, 0, pl.ds(r, 1)], sem.at[slot]).start()
            pltpu.make_async_copy(y_hbm.at[pl.ds(p2_ref[t], 1)], ybuf.at[slot, 1, pl.ds(r, 1)], sem.at[slot]).start()
            return carry
        lax.fori_loop(0, tc, body, 0, unroll=8)

    @pl.when(i == 0)
    def _():
        start_gather(0, 0)

    @pl.when(i + 1 < nsteps)
    def _():
        start_gather(i + 1, (i + 1) % 2)

    slot = i % 2
    for k in range(TOP_K):
        pltpu.make_async_copy(y_hbm.at[pl.ds(0, tc)], ybuf.at[slot, k], sem.at[slot]).wait()
    xo = x_ref[...] + (meta_ref[:, 2:3] * ybuf[slot, 0] + meta_ref[:, 3:4] * ybuf[slot, 1])
    if n0 is None:
        o0_ref[...] = xo
        o1_ref[...] = _rms(xo, g_ref[...]).astype(o1_ref.dtype)
    else:
        y = _rms(xo, g_ref[...])

        @pl.when(i < n0)
        def _():
            o0_ref[...] = y

        @pl.when(i >= n0)
        def _():
            o1_ref[...] = y


def _combine(x, meta, y_sorted, pos1, pos2, g_next, tc, split_rows=None):
    n, d = x.shape
    row = lambda i, a, b: (i, 0)
    if split_rows is None:
        n0 = None
        out_shape = (jax.ShapeDtypeStruct((n, d), F32), jax.ShapeDtypeStruct((n, d), BF16))
        out_specs = (pl.BlockSpec((tc, d), row), pl.BlockSpec((tc, d), row))
    else:
        n0 = split_rows // tc
        out_shape = (jax.ShapeDtypeStruct((split_rows, d), F32), jax.ShapeDtypeStruct((n - split_rows, d), F32))
        out_specs = (pl.BlockSpec((tc, d), lambda i, a, b: (jnp.minimum(i, n0 - 1), 0)),
                     pl.BlockSpec((tc, d), lambda i, a, b: (jnp.maximum(i - n0, 0), 0)))
    grid_spec = pltpu.PrefetchScalarGridSpec(
        num_scalar_prefetch=2,
        grid=(n // tc,),
        in_specs=[pl.BlockSpec((tc, d), row), pl.BlockSpec((tc, LANES), row),
                  pl.BlockSpec(memory_space=pl.ANY), pl.BlockSpec((1, d), lambda i, a, b: (0, 0))],
        out_specs=out_specs,
        scratch_shapes=[pltpu.VMEM((2, TOP_K, tc, d), F32), pltpu.SemaphoreType.DMA((2,))],
    )
    return pl.pallas_call(
        functools.partial(_combine_kernel, tc=tc, n0=n0),
        out_shape=out_shape,
        grid_spec=grid_spec,
        compiler_params=_params("arbitrary"),
        name="combine",
    )(pos1, pos2, x, meta, y_sorted, g_next.reshape(1, d))


def _kv_out_kernel(*refs, depth, n0, nh, hd):
    srcs = refs[:depth]
    kp_ref, vp_ref, ks_ref, vs_ref = refs[depth:]
    l, i = pl.program_id(0), pl.program_id(1)
    tm = srcs[0].shape[0]

    def copy(src, k_dst, v_dst):
        for h in range(nh):
            k_dst[pl.ds(h, tm, stride=nh), :] = src[:, h * hd:(h + 1) * hd]
            v_dst[pl.ds(h, tm, stride=nh), :] = src[:, (nh + h) * hd:(nh + h + 1) * hd]

    for ll in range(depth):
        @pl.when((l == ll) & (i < n0))
        def _(ll=ll):
            copy(srcs[ll], kp_ref, vp_ref)

        @pl.when((l == ll) & (i >= n0))
        def _(ll=ll):
            copy(srcs[ll], ks_ref, vs_ref)


def _kv_out(kvs, n_p, n_s, nh, hd, tm):
    depth = len(kvs)
    n0, n1 = n_p // tm, n_s // tm
    last = n0 + n1 - 1

    def src_spec(ll):
        return pl.BlockSpec((tm, 2 * nh * hd), lambda l, i: (jnp.where(l == ll, i, jnp.where(l < ll, 0, last)), 0))

    first = pl.BlockSpec((None, tm * nh, hd), lambda l, i: (l, jnp.minimum(i, n0 - 1), 0))
    second = pl.BlockSpec((None, tm * nh, hd), lambda l, i: (l, jnp.maximum(i - n0, 0), 0))
    shape = lambda rows: jax.ShapeDtypeStruct((depth, rows * nh, hd), F32)
    return pl.pallas_call(
        functools.partial(_kv_out_kernel, depth=depth, n0=n0, nh=nh, hd=hd),
        out_shape=(shape(n_p), shape(n_p), shape(n_s), shape(n_s)),
        grid=(depth, n0 + n1),
        in_specs=[src_spec(ll) for ll in range(depth)],
        out_specs=(first, first, second, second),
        compiler_params=_params("arbitrary", "arbitrary"),
        name="kv_out",
    )(*kvs)


def kernel(x_prompt, x_sample, cache_k, cache_v, state_conv_a, state_conv_c, page_table, norm_mix, w_in, conv_a_w,
           w_out_a, sb_bias, w_out_b, conv_c_w, conv_c_b, ln_c_g, ln_c_b, w_out_c, w_o, norm_ffn, w_router_group,
           b_router_group, w_router_expert, b_router_expert, w_gate, w_up, w_down, norm_final):
    bp, seq, d = x_prompt.shape
    db, ds, _ = x_sample.shape
    depth = w_in.shape[0]
    nh, hd = cache_k.shape[3], cache_k.shape[4]
    hw = nh * hd
    aw, cw = state_conv_a.shape[-1], state_conv_c.shape[-1]
    ng, ne = w_router_group.shape[-1], w_router_expert.shape[-1]
    kc = conv_c_w.shape[1]
    n_p, n_s = bp * seq, db * ds
    n = n_p + n_s
    c_q, c_k, c_c, c_g, c_end = 3 * aw, 3 * aw + hw, 3 * aw + 3 * hw, 3 * aw + 3 * hw + 2 * cw, w_in.shape[-1]
    tn = 512 if all(c % 512 == 0 for c in (aw, hw, cw, d)) else LANES
    tmx = _tile(math.gcd(n_p, n_s), 512, 16)
    tme = 512 if (TOP_K * n) % 512 == 0 else 64
    assert (TOP_K * n) % tme == 0
    n_tiles = TOP_K * n // tme
    tq = _tile(seq, 256, LANES)
    nq = seq // tq
    tqa = 2 * tq if seq % (2 * tq) == 0 else tq
    assert tq >= 2 * kc

    tc = _tile(math.gcd(n_p, n_s), 256, 16)
    x_parts = (x_prompt.reshape(n_p, d), x_sample.reshape(n_s, d))
    h = _rmsnorm(x_parts, norm_mix[0], BF16, tmx)
    outs = {k: [] for k in ("kv", "ap", "as", "cp", "cs")}

    for l in range(depth):
        pa = _proj(h, w_in, l, 0, c_q, tn, (BF16,), name="proj_a")
        q = _proj(h, w_in, l, c_q, c_k, tn, (BF16,), name="proj_q")
        kv, kvb = _proj(h, w_in, l, c_k, c_c, tn, (F32, BF16), name="proj_kv")
        pc = _proj(h, w_in, l, c_c, c_g, tn, (BF16,), sigmoid_from=cw // tn, name="proj_c")
        gates = _proj(h, w_in, l, c_g, c_end, tn, (BF16,), sigmoid_from=0, name="proj_gates")

        conv_w = (conv_a_w[l], conv_c_w[l], conv_c_b[l], ln_c_g[l], ln_c_b[l])
        af_p, cf_p, ha_p, hc_p = _mix_prompt(pa, pc, bp, seq, aw, cw, *conv_w)
        pa_t = pa[n_p:].reshape(db, ds, 3 * aw).transpose(1, 0, 2)
        pc_t = pc[n_p:].reshape(db, ds, 2 * cw).transpose(1, 0, 2)
        af_s, cf_s, ha_s, hc_s = _mix_sample(pa_t, pc_t, state_conv_a[l].transpose(1, 0, 2),
                                             state_conv_c[l].transpose(1, 0, 2), aw, cw, *conv_w)
        af_s = af_s.transpose(1, 0, 2).reshape(n_s, aw)
        cf_s = cf_s.transpose(1, 0, 2).reshape(n_s, cw)

        at_p = _attn_prompt(q, kvb, sb_bias[l], bp, seq, nh, hd, tqa, tq, seq // tqa, 0, BF16, False, "attn_prompt")
        q_s = q[n_p:].astype(F32).reshape(db, ds, nh // 2, 2, hd)
        zq = jnp.zeros((db, ds, nh // 2, hd), F32)
        q_bd = jnp.concatenate([jnp.concatenate([q_s[:, :, :, 0], zq], axis=-1),
                                jnp.concatenate([zq, q_s[:, :, :, 1]], axis=-1)], axis=1)
        q_bd = q_bd.transpose(0, 2, 1, 3).reshape(db, (nh // 2) * 2 * ds, 2 * hd)
        kn = kv[n_p:, :hw].reshape(db, ds, hw)
        vn = kv[n_p:, hw:].reshape(db, ds, hw)
        raw = _attn_sample(q_bd, kn, vn, cache_k, cache_v, l, page_table, sb_bias[l], ds, nh, hd)
        raw = raw.reshape(db, nh // 2, 2, ds, 2, hd)
        at_s = jnp.stack([raw[:, :, 0, :, 0], raw[:, :, 1, :, 1]], axis=2)
        at_s = at_s.transpose(0, 3, 1, 2, 4).reshape(n_s, hw).astype(BF16)

        merged = _merge((af_p, at_p, cf_p), (af_s, at_s, cf_s), gates, w_out_a[l].astype(BF16),
                        w_out_b[l].astype(BF16), w_out_c[l].astype(BF16), tmx)
        wr = jnp.zeros((d, LANES), F32).at[:, :ng].set(w_router_group[l]).at[:, ng:ng + ne].set(w_router_expert[l])
        br = jnp.zeros((1, LANES), F32).at[0, :ng].set(b_router_group[l]).at[0, ng:ng + ne].set(b_router_expert[l])
        x_prompt_in = x_parts[0][:n_p]
        x, h2, logits = _outproj(merged, x_parts, w_o[l].astype(BF16), norm_ffn[l], wr, br, tmx)

        if l < depth - 1:
            x_t = x_prompt_in.reshape(bp, seq, d)[:, seq - tq:].reshape(bp * tq, d)
            tp = lambda c0, c1, nm: _tail_proj(x_t, norm_mix[l], w_in, l, c0, c1, tn, nm)
            pa_x, q_x = tp(0, c_q, "tail_proj_a"), tp(c_q, c_k, "tail_proj_q")
            pc_x, gl_x = tp(c_c, c_g, "tail_proj_c"), tp(c_g, c_end, "tail_proj_gates")
            af_x, cf_x, _, _ = _mix_prompt(pa_x, pc_x, bp, tq, aw, cw, *conv_w, out_dtype=F32, gate_is_logit=True,
                                           name="tail_mix")
            at_x = _attn_prompt(q_x, kvb, sb_bias[l], bp, seq, nh, hd, tq, tq, 1, nq - 1, F32, True, "tail_attn")
            merged_x = _tail_merge(af_x, at_x, cf_x, gl_x, w_out_a, w_out_b, w_out_c, l, tn)
            logits_x = _tail_router(merged_x, x_t, w_o, l, norm_ffn[l], wr, br, tn)
            keep = tq - 2 * kc
            logits = logits.reshape(-1, LANES)
            for b in range(bp):
                logits = lax.dynamic_update_slice(logits, logits_x[b * tq + keep:(b + 1) * tq],
                                                  (b * seq + seq - tq + keep, 0))

        meta, counts = _route(logits, ng, ne, tmx)
        e1, e2 = meta[:, 0].astype(jnp.int32), meta[:, 1].astype(jnp.int32)
        cnt = counts[0, ng:ng + ne].astype(jnp.int32)
        ends = jnp.cumsum(cnt)
        off = ends - cnt
        pos1 = off[e1] + meta[:, 4].astype(jnp.int32)
        pos2 = off[e2] + meta[:, 5].astype(jnp.int32)
        x_sorted = _dispatch(h2, pos1, pos2)
        marks = jnp.concatenate([jnp.arange(n_tiles, dtype=jnp.int32) * tme, off])
        ids = jnp.arange(marks.shape[0], dtype=jnp.int32)
        before = (marks[None, :] < marks[:, None]) | ((marks[None, :] == marks[:, None]) & (ids[None, :] < ids[:, None]))
        rank = jnp.sum(before, axis=1).astype(jnp.int32)
        starts = jnp.sum(jnp.where(rank[None, :] == ids[:, None], marks[None, :], 0), axis=1).astype(jnp.int32)
        stops = jnp.concatenate([starts[1:], jnp.full((1,), TOP_K * n, jnp.int32)])
        item_tile = jnp.minimum(starts // tme, n_tiles - 1)
        item_expert = jnp.minimum(jnp.sum(ends[None, :] <= starts[:, None], axis=1), ne - 1).astype(jnp.int32)
        y_sorted = _experts(x_sorted, item_tile, item_expert, starts - item_tile * tme, stops - item_tile * tme,
                            w_gate, w_up, w_down, l, tme)
        if l < depth - 1:
            x, h = _combine(x, meta, y_sorted, pos1, pos2, norm_mix[l + 1], tc)
            x_parts = (x,)
        else:
            y_p, y_s = _combine(x, meta, y_sorted, pos1, pos2, norm_final, tc, split_rows=n_p)

        outs["kv"].append(kv)
        outs["ap"].append(ha_p)
        outs["as"].append(ha_s.transpose(1, 0, 2))
        outs["cp"].append(hc_p)
        outs["cs"].append(hc_s.transpose(1, 0, 2))

    k_p, v_p, k_s, v_s = _kv_out(outs["kv"], n_p, n_s, nh, hd, tmx)
    return (y_p.reshape(bp, seq, d), y_s.reshape(db, ds, d),
            k_p.reshape(depth, bp, seq, nh, hd), v_p.reshape(depth, bp, seq, nh, hd),
            k_s.reshape(depth, db, ds, nh, hd), v_s.reshape(depth, db, ds, nh, hd),
            jnp.stack(outs["ap"]), jnp.stack(outs["as"]), jnp.stack(outs["cp"]), jnp.stack(outs["cs"]))
```

```python
import functools
import math

import jax
import jax.numpy as jnp
from jax import lax
from jax.experimental import pallas as pl
from jax.experimental.pallas import tpu as pltpu

BF16 = jnp.bfloat16
F32 = jnp.float32
RMS_EPS = 1e-6
LN_EPS = 1e-5
TOP_K = 2
LANES = 128
V7X_VMEM_BUDGET = 56 * 1024 * 1024
NEG = -3.0e38
LOG2E = 1.4426950408889634
NT_DIMS = (((1,), (1,)), ((), ()))


def _tile(n, target, align):
    if n <= target:
        return n
    t = (target // align) * align
    while t >= align:
        if n % t == 0:
            return t
        t -= align
    raise ValueError(f"no tile for {n} (target {target}, align {align})")


def _params(*sem):
    return pltpu.CompilerParams(dimension_semantics=sem or None, vmem_limit_bytes=V7X_VMEM_BUDGET)


def _sigmoid(x):
    return 0.5 * jnp.tanh(0.5 * x) + 0.5


def _split_bf16(x):
    hi = x.astype(BF16)
    lo = (x - hi.astype(F32)).astype(BF16)
    return hi, lo


def _dot(a, b):
    return jnp.dot(a, b, preferred_element_type=F32)


def _dot3(a_hi, a_lo, w):
    w_hi, w_lo = _split_bf16(w)
    return _dot(a_hi, w_hi) + _dot(a_hi, w_lo) + _dot(a_lo, w_hi)


def _rms(x, g):
    ms = jnp.mean(x * x, axis=-1, keepdims=True)
    return x * lax.rsqrt(ms + RMS_EPS) * g


def _two_part_specs(parts, tm):
    if len(parts) == 1:
        return [pl.BlockSpec((tm, parts[0].shape[1]), lambda i, *_: (i, 0))], None
    n0 = parts[0].shape[0] // tm
    return [pl.BlockSpec((tm, parts[0].shape[1]), lambda i, *_: (jnp.minimum(i, n0 - 1), 0)),
            pl.BlockSpec((tm, parts[1].shape[1]), lambda i, *_: (jnp.maximum(i - n0, 0), 0))], n0


def _pick_part(refs, n0, fn):
    if n0 is None:
        fn(refs[0])
        return
    i = pl.program_id(0)

    @pl.when(i < n0)
    def _():
        fn(refs[0])

    @pl.when(i >= n0)
    def _():
        fn(refs[1])


def _rmsnorm_kernel(*refs, n0):
    g_ref, o_ref = refs[-2:]

    def run(x_ref):
        o_ref[...] = _rms(x_ref[...], g_ref[...]).astype(o_ref.dtype)

    _pick_part(refs[:-2], n0, run)


def _rmsnorm(x_parts, g, out_dtype, tm):
    n, d = sum(x.shape[0] for x in x_parts), x_parts[0].shape[1]
    specs, n0 = _two_part_specs(x_parts, tm)
    return pl.pallas_call(
        functools.partial(_rmsnorm_kernel, n0=n0),
        out_shape=jax.ShapeDtypeStruct((n, d), out_dtype),
        grid=(n // tm,),
        in_specs=specs + [pl.BlockSpec((1, d), lambda i: (0, 0))],
        out_specs=pl.BlockSpec((tm, d), lambda i: (i, 0)),
        compiler_params=_params("arbitrary"),
        name="rmsnorm",
    )(*x_parts, g.reshape(1, d))


def _proj_kernel(tiles_ref, h_ref, w_ref, *o_refs, ranges, halves):
    j = pl.program_id(1)
    tn = w_ref.shape[1]
    h = h_ref[...]
    step = tn // halves
    accs = [_dot(h, w_ref[:, c * step:(c + 1) * step].astype(BF16)) for c in range(halves)]
    for lo, hi, sig, outs in ranges:
        @pl.when((j >= lo) & (j < hi))
        def _(sig=sig, outs=outs):
            for c, acc in enumerate(accs):
                val = _sigmoid(acc) if sig else acc
                for o in outs:
                    o_refs[o][:, c * step:(c + 1) * step] = val.astype(o_refs[o].dtype)


def _proj(h, w_in, layer, tn, segments, name):
    n, k = h.shape
    tm = _tile(n, 2176, 16)
    tiles, ranges, out_shape, out_specs = [], [], [], []
    for c0, c1, dtypes, sig_from in segments:
        lo, nb = len(tiles), (c1 - c0) // tn
        tiles += [c0 // tn + b for b in range(nb)]
        outs = tuple(range(len(out_shape), len(out_shape) + len(dtypes)))
        split = nb if sig_from is None else (sig_from - c0) // tn
        if split > 0:
            ranges.append((lo, lo + split, False, outs))
        if split < nb:
            ranges.append((lo + split, lo + nb, True, outs))
        for dt in dtypes:
            out_shape.append(jax.ShapeDtypeStruct((n, c1 - c0), dt))
            out_specs.append(pl.BlockSpec((tm, tn), lambda i, j, t, lo=lo, nb=nb: (i, jnp.clip(j - lo, 0, nb - 1))))
    tile_ids = jnp.asarray(tiles, jnp.int32)
    grid_spec = pltpu.PrefetchScalarGridSpec(
        num_scalar_prefetch=1,
        grid=(n // tm, len(tiles)),
        in_specs=[pl.BlockSpec((tm, k), lambda i, j, t: (i, 0)),
                  pl.BlockSpec((None, k, tn), lambda i, j, t: (layer, 0, t[j]))],
        out_specs=out_specs,
    )
    return pl.pallas_call(
        functools.partial(_proj_kernel, ranges=tuple(ranges), halves=2 if tn % 256 == 0 else 1),
        out_shape=tuple(out_shape),
        grid_spec=grid_spec,
        compiler_params=_params("arbitrary", "arbitrary"),
        name=name,
    )(tile_ids, h, w_in)


def _tail_proj_kernel(x_ref, g_ref, w_ref, o_ref, hh_ref, hl_ref):
    @pl.when(pl.program_id(0) == 0)
    def _():
        hi, lo = _split_bf16(_rms(x_ref[...], g_ref[...]))
        hh_ref[...] = hi
        hl_ref[...] = lo

    o_ref[...] = _dot3(hh_ref[...], hl_ref[...], w_ref[...])


def _tail_proj(x_t, g, w_in, layer, c0, c1, tn, name):
    r, d = x_t.shape
    cb0 = c0 // tn
    return pl.pallas_call(
        _tail_proj_kernel,
        out_shape=jax.ShapeDtypeStruct((r, c1 - c0), F32),
        grid=((c1 - c0) // tn,),
        in_specs=[pl.BlockSpec((r, d), lambda j: (0, 0)), pl.BlockSpec((1, d), lambda j: (0, 0)),
                  pl.BlockSpec((None, d, tn), lambda j: (layer, 0, cb0 + j))],
        out_specs=pl.BlockSpec((r, tn), lambda j: (0, j)),
        scratch_shapes=[pltpu.VMEM((r, d), BF16), pltpu.VMEM((r, d), BF16)],
        compiler_params=_params("arbitrary"),
        name=name,
    )(x_t, g.reshape(1, d), w_in)


def _ln_silu(acc, g, b):
    mu = jnp.mean(acc, axis=-1, keepdims=True)
    xc = acc - mu
    var = jnp.mean(xc * xc, axis=-1, keepdims=True)
    y = xc * lax.rsqrt(var + LN_EPS) * g + b
    return y * _sigmoid(y)


def _mix_prompt_kernel(ab_ref, ac_ref, ax_ref, cv_ref, cg_ref, caw_ref, ccw_ref, ccb_ref, lng_ref, lnb_ref,
                       af_ref, cf_ref, ha_ref, hc_ref, exa_ref, exc_ref, *, tt, ka, kc, pa, pc, gate_is_logit):
    ti = pl.program_id(1)

    @pl.when(ti == 0)
    def _():
        exa_ref[0:pa, :] = jnp.zeros((pa, exa_ref.shape[1]), F32)
        exc_ref[0:pc, :] = jnp.zeros((pc, exc_ref.shape[1]), F32)

    u = ac_ref[...].astype(F32) * ax_ref[...].astype(F32)
    exa_ref[pa:pa + tt, :] = u
    conv = caw_ref[ka - 1:ka, :] * u
    for j in range(ka - 1):
        conv = conv + caw_ref[j:j + 1, :] * exa_ref[pa - (ka - 1) + j:pa - (ka - 1) + j + tt, :]
    af_ref[...] = (ab_ref[...].astype(F32) * conv).astype(af_ref.dtype)

    cg = cg_ref[...].astype(F32)
    glu = cv_ref[...].astype(F32) * (_sigmoid(cg) if gate_is_logit else cg)
    exc_ref[pc:pc + tt, :] = glu
    acc = ccb_ref[...] + ccw_ref[kc - 1:kc, :] * glu
    for j in range(kc - 1):
        acc = acc + ccw_ref[j:j + 1, :] * exc_ref[pc - (kc - 1) + j:pc - (kc - 1) + j + tt, :]
    cf_ref[...] = _ln_silu(acc, lng_ref[...], lnb_ref[...]).astype(cf_ref.dtype)

    @pl.when(ti == pl.num_programs(1) - 1)
    def _():
        ha_ref[...] = exa_ref[pa + tt - (ka - 1):pa + tt, :]
        hc_ref[...] = exc_ref[pc + tt - (kc - 1):pc + tt, :]

    exa_ref[0:pa, :] = exa_ref[tt:tt + pa, :]
    exc_ref[0:pc, :] = exc_ref[tt:tt + pc, :]


def _mix_prompt(pa_arr, pc_arr, bp, seq, aw, cw, caw, ccw, ccb, lng, lnb, out_dtype=None, gate_is_logit=False,
                name="mix_prompt"):
    out_dtype = out_dtype or BF16
    ka, kc = caw.shape[0], ccw.shape[0]
    tt = _tile(seq, 512, 32)
    nt = seq // tt
    pa, pc = 8, 32
    assert ka - 1 <= pa and kc - 1 <= pc and tt >= pc
    row = lambda c: (lambda b, t: (b * nt + t, c))
    const = lambda b, t: (0, 0)
    return pl.pallas_call(
        functools.partial(_mix_prompt_kernel, tt=tt, ka=ka, kc=kc, pa=pa, pc=pc, gate_is_logit=gate_is_logit),
        out_shape=(jax.ShapeDtypeStruct((bp * seq, aw), out_dtype), jax.ShapeDtypeStruct((bp * seq, cw), out_dtype),
                   jax.ShapeDtypeStruct((bp, ka - 1, aw), F32), jax.ShapeDtypeStruct((bp, kc - 1, cw), F32)),
        grid=(bp, nt),
        in_specs=[pl.BlockSpec((tt, aw), row(0)), pl.BlockSpec((tt, aw), row(1)), pl.BlockSpec((tt, aw), row(2)),
                  pl.BlockSpec((tt, cw), row(0)), pl.BlockSpec((tt, cw), row(1)),
                  pl.BlockSpec((ka, aw), const), pl.BlockSpec((kc, cw), const), pl.BlockSpec((1, cw), const),
                  pl.BlockSpec((1, cw), const), pl.BlockSpec((1, cw), const)],
        out_specs=(pl.BlockSpec((tt, aw), row(0)), pl.BlockSpec((tt, cw), row(0)),
                   pl.BlockSpec((None, ka - 1, aw), lambda b, t: (b, 0, 0)),
                   pl.BlockSpec((None, kc - 1, cw), lambda b, t: (b, 0, 0))),
        scratch_shapes=[pltpu.VMEM((pa + tt, aw), F32), pltpu.VMEM((pc + tt, cw), F32)],
        compiler_params=_params("arbitrary", "arbitrary"),
        name=name,
    )(pa_arr, pa_arr, pa_arr, pc_arr, pc_arr, caw, ccw, ccb.reshape(1, cw), lng.reshape(1, cw), lnb.reshape(1, cw))


def _mix_sample_kernel(pa_ref, pc_ref, ha_ref, hc_ref, caw_ref, ccw_ref, ccb_ref, lng_ref, lnb_ref,
                       af_ref, cf_ref, nha_ref, nhc_ref, *, aw, cw):
    ds = pa_ref.shape[0]
    ka, kc = caw_ref.shape[0], ccw_ref.shape[0]
    u = [pa_ref[t, :, aw:2 * aw].astype(F32) * pa_ref[t, :, 2 * aw:3 * aw].astype(F32) for t in range(ds)]
    glu = [pc_ref[t, :, 0:cw].astype(F32) * pc_ref[t, :, cw:2 * cw].astype(F32) for t in range(ds)]
    full_a = lambda i: ha_ref[i] if i < ka - 1 else u[i - (ka - 1)]
    full_c = lambda i: hc_ref[i] if i < kc - 1 else glu[i - (kc - 1)]
    for t in range(ds):
        conv = caw_ref[0:1, :] * full_a(t)
        for j in range(1, ka):
            conv = conv + caw_ref[j:j + 1, :] * full_a(t + j)
        af_ref[t] = (pa_ref[t, :, 0:aw].astype(F32) * conv).astype(af_ref.dtype)
        acc = ccb_ref[...] + ccw_ref[0:1, :] * full_c(t)
        for j in range(1, kc):
            acc = acc + ccw_ref[j:j + 1, :] * full_c(t + j)
        cf_ref[t] = _ln_silu(acc, lng_ref[...], lnb_ref[...]).astype(cf_ref.dtype)
    for i in range(ka - 1):
        nha_ref[i] = full_a(i + ds)
    for i in range(kc - 1):
        nhc_ref[i] = full_c(i + ds)


def _mix_sample(pa_t, pc_t, ha_t, hc_t, aw, cw, caw, ccw, ccb, lng, lnb):
    ds, db, _ = pa_t.shape
    ka, kc = caw.shape[0], ccw.shape[0]
    return pl.pallas_call(
        functools.partial(_mix_sample_kernel, aw=aw, cw=cw),
        out_shape=(jax.ShapeDtypeStruct((ds, db, aw), BF16), jax.ShapeDtypeStruct((ds, db, cw), BF16),
                   jax.ShapeDtypeStruct((ka - 1, db, aw), F32), jax.ShapeDtypeStruct((kc - 1, db, cw), F32)),
        compiler_params=_params(),
        name="mix_sample",
    )(pa_t, pc_t, ha_t, hc_t, caw, ccw, ccb.reshape(1, cw), lng.reshape(1, cw), lnb.reshape(1, cw))


def _suffix_matrix(bk):
    j = lax.broadcasted_iota(jnp.int32, (bk, bk), 0)
    s = lax.broadcasted_iota(jnp.int32, (bk, bk), 1)
    return (j >= s).astype(BF16)


def _sb_block(z2, mask, tmat, carry, two_pass):
    bk = z2.shape[1]
    sp = jnp.maximum(z2, 0.0) + jnp.log2(1.0 + jnp.exp2(-jnp.abs(z2)))
    if mask is not None:
        sp = jnp.where(mask, sp, 0.0)
    if two_pass:
        hi, lo = _split_bf16(sp)
        suf = _dot(hi, tmat) + _dot(lo, tmat)
    else:
        suf = _dot(sp.astype(BF16), tmat)
    c = suf + jnp.concatenate([carry] * (bk // LANES), axis=1)
    a = jnp.exp2(z2 - c)
    if mask is not None:
        a = jnp.where(mask, a, 0.0)
    return a, carry + jnp.broadcast_to(suf[:, 0:1], carry.shape)


def _attn_prompt_kernel(bias_ref, q_ref, k_ref, v_ref, tmat_ref, o_ref, acc_ref, carry_ref,
                        *, tq, bk, hd, hg, qi_base, scale, precise):
    g = pl.program_id(1)
    q0 = (qi_base + pl.program_id(2)) * tq
    tmat = tmat_ref[...]
    lanes = lambda h: slice(h * hd, (h + 1) * hd)

    qs = []
    for h in range(hg):
        qh = q_ref[:, lanes(h)]
        qs.append(_split_bf16(qh) if precise else (qh.astype(BF16),))

    def key_block(kb, mask, first):
        s0 = pl.multiple_of(kb * bk, bk)
        z2, sp = [], []
        for h in range(hg):
            kblk = k_ref[pl.ds(s0, bk), lanes(h)]
            s = sum(lax.dot_general(part, kblk, NT_DIMS, preferred_element_type=F32) for part in qs[h])
            z = s * (scale * LOG2E) + bias_ref[g * hg + h] * LOG2E
            soft = jnp.maximum(z, 0.0) + jnp.log2(1.0 + jnp.exp2(-jnp.abs(z)))
            z2.append(z)
            sp.append(soft if mask is None else jnp.where(mask, soft, 0.0))
        sp = jnp.concatenate(sp, axis=0)
        if precise:
            hi, lo = _split_bf16(sp)
            suf_all = _dot(hi, tmat) + _dot(lo, tmat)
        else:
            suf_all = _dot(sp.astype(BF16), tmat)
        for h in range(hg):
            suf = suf_all[h * tq:(h + 1) * tq]
            carry = jnp.zeros((tq, LANES), F32) if first else carry_ref[h]
            a = jnp.exp2(z2[h] - (suf + jnp.concatenate([carry] * (bk // LANES), axis=1)))
            if mask is not None:
                a = jnp.where(mask, a, 0.0)
            pv = _dot(a.astype(BF16), v_ref[pl.ds(s0, bk), lanes(h)])
            if first:
                acc_ref[:, lanes(h)] = pv
            else:
                acc_ref[:, lanes(h)] += pv
            carry_ref[h] = carry + jnp.broadcast_to(suf[:, 0:1], carry.shape)

    nfull = q0 // bk
    r = lax.broadcasted_iota(jnp.int32, (tq, bk), 0)
    c = lax.broadcasted_iota(jnp.int32, (tq, bk), 1)
    ndiag = tq // bk
    for d in range(ndiag - 1, -1, -1):
        key_block(nfull + d, c + d * bk < r, d == ndiag - 1)

    def body(it, _):
        key_block(nfull - 1 - 2 * it, None, False)
        key_block(nfull - 2 - 2 * it, None, False)
        return 0

    lax.fori_loop(0, nfull // 2, body, 0)

    @pl.when(nfull % 2 == 1)
    def _():
        key_block(0, None, False)

    o_ref[...] = acc_ref[...].astype(o_ref.dtype)


def _attn_prompt(q, kvb, bias, bp, seq, nh, hd, tq, bk, nq, qi_base, out_dtype, precise, name):
    hg = 4 if nh % 4 == 0 else 2
    ng = nh // hg
    assert tq % bk == 0 and seq % tq == 0
    return pl.pallas_call(
        functools.partial(_attn_prompt_kernel, tq=tq, bk=bk, hd=hd, hg=hg, qi_base=qi_base, scale=hd ** -0.5,
                          precise=precise),
        out_shape=jax.ShapeDtypeStruct((bp * nq * tq, nh * hd), out_dtype),
        grid=(bp, ng, nq),
        in_specs=[pl.BlockSpec(memory_space=pltpu.SMEM),
                  pl.BlockSpec((tq, hg * hd), lambda b, g, i: (b * nq + i, g)),
                  pl.BlockSpec((seq, hg * hd), lambda b, g, i: (b, g)),
                  pl.BlockSpec((seq, hg * hd), lambda b, g, i: (b, ng + g)),
                  pl.BlockSpec((bk, bk), lambda b, g, i: (0, 0))],
        out_specs=pl.BlockSpec((tq, hg * hd), lambda b, g, i: (b * nq + i, g)),
        scratch_shapes=[pltpu.VMEM((tq, hg * hd), F32), pltpu.VMEM((hg, tq, LANES), F32)],
        compiler_params=_params("arbitrary", "arbitrary", "arbitrary"),
        name=name,
    )(bias, q, kvb, kvb, _suffix_matrix(bk))


def _attn_sample_kernel(pt_ref, bias_ref, q_ref, kn_ref, vn_ref, tmat_ref, ck_hbm, cv_hbm, o_ref,
                        knf_ref, vnf_ref, kbuf, vbuf, sem, *, layer, npg, ds, hd, nh, page, scale):
    n = pl.program_id(0)
    rr, ww = 2 * ds, 2 * hd
    p2 = nh // 2
    rows = p2 * rr

    def page_copies(seq_idx, slot):
        for pg in range(npg):
            src = pt_ref[seq_idx * npg + pg]
            yield pltpu.make_async_copy(ck_hbm.at[layer, src], kbuf.at[slot, pg], sem.at[slot])
            yield pltpu.make_async_copy(cv_hbm.at[layer, src], vbuf.at[slot, pg], sem.at[slot])

    @pl.when(n == 0)
    def _():
        for cp in page_copies(0, 0):
            cp.start()

    @pl.when(n + 1 < pl.num_programs(0))
    def _():
        for cp in page_copies(n + 1, (n + 1) % 2):
            cp.start()

    slot = n % 2
    for cp in page_copies(n, slot):
        cp.wait()

    knf_ref[...] = jnp.zeros(knf_ref.shape, F32)
    vnf_ref[...] = jnp.zeros(vnf_ref.shape, F32)
    knf_ref[0:ds, :] = kn_ref[...]
    vnf_ref[0:ds, :] = vn_ref[...]

    row = lax.broadcasted_iota(jnp.int32, (rows, LANES), 0)
    col = lax.broadcasted_iota(jnp.int32, (rows, LANES), 1)
    head = (row // rr) * 2 + (row % rr) // ds
    bias = jnp.zeros((rows, LANES), F32)
    for hh in range(nh):
        bias = jnp.where(head == hh, bias_ref[hh] * LOG2E, bias)
    new_mask = col < (row % ds)

    lhs = [q_ref[p * rr:(p + 1) * rr, :].astype(BF16) for p in range(p2)]

    def new_tile(ref, p):
        return ref[:, p * ww:(p + 1) * ww].astype(BF16)

    def page_tile(buf, pages, p):
        per_page = [jnp.concatenate([buf.at[slot, pg][pl.ds(2 * p + e, page, stride=nh), :] for e in range(2)], axis=1)
                    for pg in pages]
        return jnp.concatenate(per_page, axis=0).astype(BF16)

    def block(k_tile, v_tile, nkeys, mask, carry, acc):
        s = jnp.concatenate(
            [lax.dot_general(lhs[p], k_tile(p), NT_DIMS, preferred_element_type=F32) for p in range(p2)], axis=0)
        z2 = s * (scale * LOG2E) + jnp.concatenate([bias] * (nkeys // LANES), axis=1)
        a, carry = _sb_block(z2, mask, tmat_ref[0:nkeys, 0:nkeys], carry, True)
        acc = [acc[p] + _dot(a[p * rr:(p + 1) * rr, :].astype(BF16), v_tile(p)) for p in range(p2)]
        return carry, acc

    carry = jnp.zeros((rows, LANES), F32)
    acc = [jnp.zeros((rr, ww), F32) for _ in range(p2)]
    carry, acc = block(functools.partial(new_tile, knf_ref), functools.partial(new_tile, vnf_ref), page, new_mask,
                       carry, acc)
    hi_pg = npg
    while hi_pg > 0:
        pages = list(range(max(hi_pg - 2, 0), hi_pg))
        carry, acc = block(functools.partial(page_tile, kbuf, pages), functools.partial(page_tile, vbuf, pages),
                           len(pages) * page, None, carry, acc)
        hi_pg -= len(pages)
    for p in range(p2):
        o_ref[p * rr:(p + 1) * rr, :] = acc[p]


def _attn_sample(q_bd, kn, vn, cache_k, cache_v, layer, page_table, bias, ds, nh, hd):
    db = q_bd.shape[0]
    npg = page_table.shape[1]
    depth, npool, page = cache_k.shape[:3]
    hw = nh * hd
    assert page == LANES and hd == LANES and (2 * ds) % 8 == 0 and nh % 2 == 0
    ck = cache_k.reshape(depth, npool, page * nh, hd)
    cv = cache_v.reshape(depth, npool, page * nh, hd)
    rows = (nh // 2) * 2 * ds
    seq_blk = lambda shape: pl.BlockSpec((None,) + shape, lambda n, pt: (n, 0, 0))
    grid_spec = pltpu.PrefetchScalarGridSpec(
        num_scalar_prefetch=1,
        grid=(db,),
        in_specs=[pl.BlockSpec(memory_space=pltpu.SMEM), seq_blk((rows, 2 * hd)), seq_blk((ds, hw)), seq_blk((ds, hw)),
                  pl.BlockSpec((2 * page, 2 * page), lambda n, pt: (0, 0)),
                  pl.BlockSpec(memory_space=pl.ANY), pl.BlockSpec(memory_space=pl.ANY)],
        out_specs=seq_blk((rows, 2 * hd)),
        scratch_shapes=[pltpu.VMEM((page, hw), F32), pltpu.VMEM((page, hw), F32),
                        pltpu.VMEM((2, npg, page * nh, hd), F32), pltpu.VMEM((2, npg, page * nh, hd), F32),
                        pltpu.SemaphoreType.DMA((2,))],
    )
    return pl.pallas_call(
        functools.partial(_attn_sample_kernel, layer=layer, npg=npg, ds=ds, hd=hd, nh=nh, page=page,
                          scale=hd ** -0.5),
        out_shape=jax.ShapeDtypeStruct((db, rows, 2 * hd), F32),
        grid_spec=grid_spec,
        compiler_params=_params("arbitrary"),
        name="attn_sample",
    )(page_table.reshape(-1), bias, q_bd, kn, vn, _suffix_matrix(2 * page), ck, cv)


def _merge_kernel(afp_ref, atp_ref, cfp_ref, afs_ref, ats_ref, cfs_ref, g0_ref, g1_ref, g2_ref,
                  wa_ref, wb_ref, wc_ref, o_ref, *, n_prompt_tiles):
    def run(af_ref, at_ref, cf_ref):
        ya = _dot(af_ref[...], wa_ref[...])
        yb = _dot(at_ref[...], wb_ref[...])
        yc = _dot(cf_ref[...], wc_ref[...])
        o_ref[...] = (g0_ref[...].astype(F32) * ya + g1_ref[...].astype(F32) * yb
                      + g2_ref[...].astype(F32) * yc).astype(o_ref.dtype)

    i = pl.program_id(0)

    @pl.when(i < n_prompt_tiles)
    def _():
        run(afp_ref, atp_ref, cfp_ref)

    @pl.when(i >= n_prompt_tiles)
    def _():
        run(afs_ref, ats_ref, cfs_ref)


def _merge(feats_p, feats_s, gates, wa, wb, wc, tm):
    n_p, n_s = feats_p[0].shape[0], feats_s[0].shape[0]
    d = wa.shape[1]
    npt, nst = n_p // tm, n_s // tm
    prow = lambda a: pl.BlockSpec((tm, a.shape[1]), lambda i: (jnp.minimum(i, npt - 1), 0))
    srow = lambda a: pl.BlockSpec((tm, a.shape[1]), lambda i: (jnp.maximum(i - npt, 0), 0))
    gate = lambda k: pl.BlockSpec((tm, d), lambda i: (i, k))
    resident = lambda w: pl.BlockSpec(w.shape, lambda i: (0, 0), pipeline_mode=pl.Buffered(1))
    return pl.pallas_call(
        functools.partial(_merge_kernel, n_prompt_tiles=npt),
        out_shape=jax.ShapeDtypeStruct((n_p + n_s, d), BF16),
        grid=(npt + nst,),
        in_specs=[prow(a) for a in feats_p] + [srow(a) for a in feats_s] + [gate(0), gate(1), gate(2),
                  resident(wa), resident(wb), resident(wc)],
        out_specs=pl.BlockSpec((tm, d), lambda i: (i, 0)),
        compiler_params=_params("arbitrary"),
        name="merge",
    )(*feats_p, *feats_s, gates, gates, gates, wa, wb, wc)


def _router_logits(h, wr, br):
    hh, hl = _split_bf16(h)
    return _dot3(hh, hl, wr) + br


def _outproj_kernel(m_ref, *refs, n0):
    wo_ref, g_ref, wr_ref, br_ref, xo_ref, h_ref, lg_ref = refs[-7:]

    def run(x_ref):
        xn = x_ref[...] + _dot(m_ref[...], wo_ref[...])
        xo_ref[...] = xn
        h = _rms(xn, g_ref[...])
        h_ref[...] = h
        lg_ref[...] = _router_logits(h, wr_ref[...], br_ref[...])

    _pick_part(refs[:-7], n0, run)


def _outproj(merged, x_parts, wo, g, wr, br, tm):
    n, d = merged.shape
    rowb = lambda w: pl.BlockSpec((tm, w), lambda i: (i, 0))
    const = lambda shape: pl.BlockSpec(shape, lambda i: (0, 0))
    x_specs, n0 = _two_part_specs(x_parts, tm)
    return pl.pallas_call(
        functools.partial(_outproj_kernel, n0=n0),
        out_shape=(jax.ShapeDtypeStruct((n, d), F32), jax.ShapeDtypeStruct((n, d), F32),
                   jax.ShapeDtypeStruct((n, LANES), F32)),
        grid=(n // tm,),
        in_specs=[rowb(d)] + x_specs + [pl.BlockSpec((d, d), lambda i: (0, 0), pipeline_mode=pl.Buffered(1)),
                                        const((1, d)), const((d, LANES)), const((1, LANES))],
        out_specs=(rowb(d), rowb(d), rowb(LANES)),
        compiler_params=_params("arbitrary"),
        name="outproj",
    )(merged, *x_parts, wo, g.reshape(1, d), wr, br)


def _tail_merge_kernel(af_ref, at_ref, cf_ref, g0_ref, g1_ref, g2_ref, wa_ref, wb_ref, wc_ref, o_ref):
    y = [_dot3(*_split_bf16(f_ref[...]), w_ref[...])
         for f_ref, w_ref in ((af_ref, wa_ref), (at_ref, wb_ref), (cf_ref, wc_ref))]
    o_ref[...] = (_sigmoid(g0_ref[...]) * y[0] + _sigmoid(g1_ref[...]) * y[1]
                  + _sigmoid(g2_ref[...]) * y[2])


def _tail_merge(af, at, cf, gate_logits, wa, wb, wc, layer, tn):
    r, d = af.shape[0], wa.shape[-1]
    full = lambda a: pl.BlockSpec(a.shape, lambda j: (0, 0))
    gate = lambda k: pl.BlockSpec((r, tn), lambda j: (0, k * (d // tn) + j))
    wcol = lambda w: pl.BlockSpec((None, w.shape[1], tn), lambda j: (layer, 0, j))
    return pl.pallas_call(
        _tail_merge_kernel,
        out_shape=jax.ShapeDtypeStruct((r, d), F32),
        grid=(d // tn,),
        in_specs=[full(af), full(at), full(cf), gate(0), gate(1), gate(2), wcol(wa), wcol(wb), wcol(wc)],
        out_specs=pl.BlockSpec((r, tn), lambda j: (0, j)),
        compiler_params=_params("arbitrary"),
        name="tail_merge",
    )(af, at, cf, gate_logits, gate_logits, gate_logits, wa, wb, wc)


def _tail_router_kernel(m_ref, x_ref, wo_ref, g_ref, wr_ref, br_ref, lg_ref, mh_ref, ml_ref, xn_ref, *, tn):
    j = pl.program_id(0)

    @pl.when(j == 0)
    def _():
        hi, lo = _split_bf16(m_ref[...])
        mh_ref[...] = hi
        ml_ref[...] = lo

    col = pl.multiple_of(j * tn, tn)
    xn_ref[:, pl.ds(col, tn)] = x_ref[...] + _dot3(mh_ref[...], ml_ref[...], wo_ref[...])

    @pl.when(j == pl.num_programs(0) - 1)
    def _():
        lg_ref[...] = _router_logits(_rms(xn_ref[...], g_ref[...]), wr_ref[...], br_ref[...])


def _tail_router(merged, x_t, wo, layer, g, wr, br, tn):
    r, d = x_t.shape
    const = lambda shape: pl.BlockSpec(shape, lambda j: (0, 0))
    return pl.pallas_call(
        functools.partial(_tail_router_kernel, tn=tn),
        out_shape=jax.ShapeDtypeStruct((r, LANES), F32),
        grid=(d // tn,),
        in_specs=[const((r, d)), pl.BlockSpec((r, tn), lambda j: (0, j)),
                  pl.BlockSpec((None, d, tn), lambda j: (layer, 0, j)), const((1, d)), const((d, LANES)),
                  const((1, LANES))],
        out_specs=const((r, LANES)),
        scratch_shapes=[pltpu.VMEM((r, d), BF16), pltpu.VMEM((r, d), BF16), pltpu.VMEM((r, d), F32)],
        compiler_params=_params("arbitrary"),
        name="tail_router",
    )(merged, x_t, wo, g.reshape(1, d), wr, br)


def _route_kernel(lg_ref, tri_ref, meta_ref, cnt_ref, run_ref, *, ng, ne):
    i = pl.program_id(0)

    @pl.when(i == 0)
    def _():
        run_ref[...] = jnp.zeros(run_ref.shape, F32)

    x = lg_ref[...]
    epg = ne // ng
    lane = lax.broadcasted_iota(jnp.int32, x.shape, 1).astype(F32)
    rmax = lambda v: jnp.max(v, axis=1, keepdims=True)
    rmin = lambda v: jnp.min(v, axis=1, keepdims=True)
    rsum = lambda v: jnp.sum(v, axis=1, keepdims=True)

    is_g = lane < ng
    gl = jnp.where(is_g, x, NEG)
    gmax = rmax(gl)
    gidx = rmin(jnp.where(is_g & (gl == gmax), lane, float(LANES)))
    gw = 1.0 / rsum(jnp.where(is_g, jnp.exp(jnp.minimum(x - gmax, 0.0)), 0.0))

    lo = ng + gidx * epg
    in_grp = (lane >= lo) & (lane < lo + epg)
    el = jnp.where(in_grp, x, NEG)
    v1 = rmax(el)
    i1 = rmin(jnp.where(in_grp & (el == v1), lane, float(LANES)))
    rest = in_grp & (lane != i1)
    el2 = jnp.where(rest, x, NEG)
    v2 = rmax(el2)
    i2 = rmin(jnp.where(rest & (el2 == v2), lane, float(LANES)))
    t = jnp.exp(v2 - v1)
    w1 = gw / (1.0 + t)
    w2 = gw * t / (1.0 + t)

    sel1, sel2 = lane == i1, lane == i2
    onehot = (sel1 | sel2).astype(BF16)
    rank = _dot(tri_ref[...], onehot) + run_ref[...]
    r1 = rsum(jnp.where(sel1, rank, 0.0))
    r2 = rsum(jnp.where(sel2, rank, 0.0))
    run_ref[...] += jnp.sum(onehot.astype(F32), axis=0, keepdims=True)
    cnt_ref[...] = run_ref[...]

    vals = [i1 - ng, i2 - ng, w1, w2, r1, r2]
    meta = jnp.zeros(x.shape, F32)
    for k, v in enumerate(vals):
        meta = jnp.where(lane == k, v, meta)
    meta_ref[...] = meta


def _route(logits, ng, ne, tr):
    n = logits.shape[0]
    r = lax.broadcasted_iota(jnp.int32, (tr, tr), 0)
    c = lax.broadcasted_iota(jnp.int32, (tr, tr), 1)
    tri = (c < r).astype(BF16)
    return pl.pallas_call(
        functools.partial(_route_kernel, ng=ng, ne=ne),
        out_shape=(jax.ShapeDtypeStruct((n, LANES), F32), jax.ShapeDtypeStruct((1, LANES), F32)),
        grid=(n // tr,),
        in_specs=[pl.BlockSpec((tr, LANES), lambda i: (i, 0)), pl.BlockSpec((tr, tr), lambda i: (0, 0))],
        out_specs=(pl.BlockSpec((tr, LANES), lambda i: (i, 0)), pl.BlockSpec((1, LANES), lambda i: (0, 0))),
        scratch_shapes=[pltpu.VMEM((1, LANES), F32)],
        compiler_params=_params("arbitrary"),
        name="route",
    )(logits, tri)


def _dispatch_kernel(p1_ref, p2_ref, h_ref, xs_hbm, sem, *, tc):
    i = pl.program_id(0)

    def row(r, carry):
        t = i * tc + r
        pltpu.make_async_copy(h_ref.at[pl.ds(r, 1)], xs_hbm.at[pl.ds(p1_ref[t], 1)], sem).start(priority=0)
        pltpu.make_async_copy(h_ref.at[pl.ds(r, 1)], xs_hbm.at[pl.ds(p2_ref[t], 1)], sem).start(priority=1)
        return carry

    lax.fori_loop(0, tc, row, 0, unroll=8)
    for _ in range(TOP_K):
        pltpu.make_async_copy(h_ref, xs_hbm.at[pl.ds(0, tc)], sem).wait()


def _dispatch(h2, pos1, pos2):
    n, d = h2.shape
    tc = _tile(n, 256, 8)
    grid_spec = pltpu.PrefetchScalarGridSpec(
        num_scalar_prefetch=2,
        grid=(n // tc,),
        in_specs=[pl.BlockSpec((tc, d), lambda i, a, b: (i, 0))],
        out_specs=pl.BlockSpec(memory_space=pl.ANY),
        scratch_shapes=[pltpu.SemaphoreType.DMA(())],
    )
    return pl.pallas_call(
        functools.partial(_dispatch_kernel, tc=tc),
        out_shape=jax.ShapeDtypeStruct((TOP_K * n, d), h2.dtype),
        grid_spec=grid_spec,
        compiler_params=_params("arbitrary"),
        name="dispatch",
    )(pos1, pos2, h2)


def _expert_kernel(it_ref, ie_ref, lo_ref, hi_ref, x_ref, wg_ref, wu_ref, wd_ref, y_ref, wgb, wub, wdb, cur_ref):
    g = pl.program_id(0)
    first_of_tile = (g == 0) | (it_ref[g] != it_ref[jnp.maximum(g - 1, 0)])
    lo, hi = lo_ref[g], hi_ref[g]

    @pl.when(g == 0)
    def _():
        cur_ref[0] = -1

    @pl.when(hi > lo)
    def _():
        @pl.when(cur_ref[0] != ie_ref[g])
        def _():
            wgb[...] = wg_ref[...].astype(BF16)
            wub[...] = wu_ref[...].astype(BF16)
            wdb[...] = wd_ref[...].astype(BF16)
            cur_ref[0] = ie_ref[g]

        x = x_ref[...].astype(BF16)
        gate = _dot(x, wgb[...])
        act = gate * _sigmoid(gate) * _dot(x, wub[...])
        y = _dot(act.astype(BF16), wdb[...])
        row = lax.broadcasted_iota(jnp.int32, y.shape, 0)
        y = jnp.where((row >= lo) & (row < hi), y, 0.0)

        @pl.when(first_of_tile)
        def _():
            y_ref[...] = y

        @pl.when(jnp.logical_not(first_of_tile))
        def _():
            y_ref[...] += y

    @pl.when((hi <= lo) & first_of_tile)
    def _():
        y_ref[...] = jnp.zeros(y_ref.shape, F32)


def _experts(x_sorted, item_tile, item_expert, item_lo, item_hi, w_gate, w_up, w_down, layer, tm):
    rows, d = x_sorted.shape
    f = w_gate.shape[-1]
    wspec = lambda a, b: pl.BlockSpec((None, None, a, b), lambda g, it, ie, lo, hi: (layer, ie[g], 0, 0))
    grid_spec = pltpu.PrefetchScalarGridSpec(
        num_scalar_prefetch=4,
        grid=(item_tile.shape[0],),
        in_specs=[pl.BlockSpec((tm, d), lambda g, it, ie, lo, hi: (it[g], 0)), wspec(d, f), wspec(d, f), wspec(f, d)],
        out_specs=pl.BlockSpec((tm, d), lambda g, it, ie, lo, hi: (it[g], 0)),
        scratch_shapes=[pltpu.VMEM((d, f), BF16), pltpu.VMEM((d, f), BF16), pltpu.VMEM((f, d), BF16),
                        pltpu.SMEM((1,), jnp.int32)],
    )
    return pl.pallas_call(
        _expert_kernel,
        out_shape=jax.ShapeDtypeStruct((rows, d), F32),
        grid_spec=grid_spec,
        compiler_params=_params("arbitrary"),
        name="experts",
    )(item_tile, item_expert, item_lo, item_hi, x_sorted, w_gate, w_up, w_down)


def _combine_kernel(p1_ref, p2_ref, x_ref, meta_ref, y_hbm, g_ref, o0_ref, o1_ref, ybuf, sem, *, tc, n0):
    i = pl.program_id(0)
    nsteps = pl.num_programs(0)

    def start_gather(tile, slot):
        def body(r, carry):
            t = tile * tc + r
            pltpu.make_async_copy(y_hbm.at[pl.ds(p1_ref[t], 1)], ybuf.at[slot, 0, pl.ds(r, 1)],
                                  sem.at[slot]).start(priority=0)
            pltpu.make_async_copy(y_hbm.at[pl.ds(p2_ref[t], 1)], ybuf.at[slot, 1, pl.ds(r, 1)],
                                  sem.at[slot]).start(priority=1)
            return carry
        lax.fori_loop(0, tc, body, 0, unroll=8)

    @pl.when(i == 0)
    def _():
        start_gather(0, 0)

    @pl.when(i + 1 < nsteps)
    def _():
        start_gather(i + 1, (i + 1) % 2)

    slot = i % 2
    for k in range(TOP_K):
        pltpu.make_async_copy(y_hbm.at[pl.ds(0, tc)], ybuf.at[slot, k], sem.at[slot]).wait()
    xo = x_ref[...] + (meta_ref[:, 2:3] * ybuf[slot, 0] + meta_ref[:, 3:4] * ybuf[slot, 1])
    if n0 is None:
        o0_ref[...] = xo
        o1_ref[...] = _rms(xo, g_ref[...]).astype(o1_ref.dtype)
    else:
        y = _rms(xo, g_ref[...])

        @pl.when(i < n0)
        def _():
            o0_ref[...] = y

        @pl.when(i >= n0)
        def _():
            o1_ref[...] = y


def _combine(x, meta, y_sorted, pos1, pos2, g_next, tc, split_rows=None):
    n, d = x.shape
    row = lambda i, a, b: (i, 0)
    if split_rows is None:
        n0 = None
        out_shape = (jax.ShapeDtypeStruct((n, d), F32), jax.ShapeDtypeStruct((n, d), BF16))
        out_specs = (pl.BlockSpec((tc, d), row), pl.BlockSpec((tc, d), row))
    else:
        n0 = split_rows // tc
        out_shape = (jax.ShapeDtypeStruct((split_rows, d), F32), jax.ShapeDtypeStruct((n - split_rows, d), F32))
        out_specs = (pl.BlockSpec((tc, d), lambda i, a, b: (jnp.minimum(i, n0 - 1), 0)),
                     pl.BlockSpec((tc, d), lambda i, a, b: (jnp.maximum(i - n0, 0), 0)))
    grid_spec = pltpu.PrefetchScalarGridSpec(
        num_scalar_prefetch=2,
        grid=(n // tc,),
        in_specs=[pl.BlockSpec((tc, d), row), pl.BlockSpec((tc, LANES), row),
                  pl.BlockSpec(memory_space=pl.ANY), pl.BlockSpec((1, d), lambda i, a, b: (0, 0))],
        out_specs=out_specs,
        scratch_shapes=[pltpu.VMEM((2, TOP_K, tc, d), F32), pltpu.SemaphoreType.DMA((2,))],
    )
    return pl.pallas_call(
        functools.partial(_combine_kernel, tc=tc, n0=n0),
        out_shape=out_shape,
        grid_spec=grid_spec,
        compiler_params=_params("arbitrary"),
        name="combine",
    )(pos1, pos2, x, meta, y_sorted, g_next.reshape(1, d))


def _kv_out_kernel(*refs, depth, n0, nh, hd):
    srcs = refs[:depth]
    kp_ref, vp_ref, ks_ref, vs_ref = refs[depth:]
    l, i = pl.program_id(0), pl.program_id(1)
    tm = srcs[0].shape[0]

    def copy(src, k_dst, v_dst):
        for h in range(nh):
            k_dst[pl.ds(h, tm, stride=nh), :] = src[:, h * hd:(h + 1) * hd]
            v_dst[pl.ds(h, tm, stride=nh), :] = src[:, (nh + h) * hd:(nh + h + 1) * hd]

    for ll in range(depth):
        @pl.when((l == ll) & (i < n0))
        def _(ll=ll):
            copy(srcs[ll], kp_ref, vp_ref)

        @pl.when((l == ll) & (i >= n0))
        def _(ll=ll):
            copy(srcs[ll], ks_ref, vs_ref)


def _kv_out(kvs, n_p, n_s, nh, hd, tm):
    depth = len(kvs)
    n0, n1 = n_p // tm, n_s // tm
    last = n0 + n1 - 1

    def src_spec(ll):
        return pl.BlockSpec((tm, 2 * nh * hd), lambda l, i: (jnp.where(l == ll, i, jnp.where(l < ll, 0, last)), 0))

    first = pl.BlockSpec((None, tm * nh, hd), lambda l, i: (l, jnp.minimum(i, n0 - 1), 0))
    second = pl.BlockSpec((None, tm * nh, hd), lambda l, i: (l, jnp.maximum(i - n0, 0), 0))
    shape = lambda rows: jax.ShapeDtypeStruct((depth, rows * nh, hd), F32)
    return pl.pallas_call(
        functools.partial(_kv_out_kernel, depth=depth, n0=n0, nh=nh, hd=hd),
        out_shape=(shape(n_p), shape(n_p), shape(n_s), shape(n_s)),
        grid=(depth, n0 + n1),
        in_specs=[src_spec(ll) for ll in range(depth)],
        out_specs=(first, first, second, second),
        compiler_params=_params("arbitrary", "arbitrary"),
        name="kv_out",
    )(*kvs)


def kernel(x_prompt, x_sample, cache_k, cache_v, state_conv_a, state_conv_c, page_table, norm_mix, w_in, conv_a_w,
           w_out_a, sb_bias, w_out_b, conv_c_w, conv_c_b, ln_c_g, ln_c_b, w_out_c, w_o, norm_ffn, w_router_group,
           b_router_group, w_router_expert, b_router_expert, w_gate, w_up, w_down, norm_final):
    bp, seq, d = x_prompt.shape
    db, ds, _ = x_sample.shape
    depth = w_in.shape[0]
    nh, hd = cache_k.shape[3], cache_k.shape[4]
    hw = nh * hd
    aw, cw = state_conv_a.shape[-1], state_conv_c.shape[-1]
    ng, ne = w_router_group.shape[-1], w_router_expert.shape[-1]
    kc = conv_c_w.shape[1]
    n_p, n_s = bp * seq, db * ds
    n = n_p + n_s
    c_q, c_k, c_c, c_g, c_end = 3 * aw, 3 * aw + hw, 3 * aw + 3 * hw, 3 * aw + 3 * hw + 2 * cw, w_in.shape[-1]
    tn = 512 if all(c % 512 == 0 for c in (aw, hw, cw, d)) else LANES
    tmx = _tile(math.gcd(n_p, n_s), 512, 16)
    tme = 512 if (TOP_K * n) % 512 == 0 else 64
    assert (TOP_K * n) % tme == 0
    n_tiles = TOP_K * n // tme
    tq = _tile(seq, 256, LANES)
    nq = seq // tq
    tqa = 2 * tq if seq % (2 * tq) == 0 else tq
    assert tq >= 2 * kc

    tc = _tile(math.gcd(n_p, n_s), 256, 16)
    x_parts = (x_prompt.reshape(n_p, d), x_sample.reshape(n_s, d))
    h = _rmsnorm(x_parts, norm_mix[0], BF16, tmx)
    outs = {k: [] for k in ("kv", "ap", "as", "cp", "cs")}

    for l in range(depth):
        pa, q, pc, gates = _proj(h, w_in, l, tn, [(0, c_q, (BF16,), None), (c_q, c_k, (BF16,), None),
                                                  (c_c, c_g, (BF16,), c_c + cw), (c_g, c_end, (BF16,), c_g)],
                                 "proj_mix")
        kv, kvb = _proj(h, w_in, l, tn, [(c_k, c_c, (F32, BF16), None)], "proj_kv")

        conv_w = (conv_a_w[l], conv_c_w[l], conv_c_b[l], ln_c_g[l], ln_c_b[l])
        af_p, cf_p, ha_p, hc_p = _mix_prompt(pa, pc, bp, seq, aw, cw, *conv_w)
        pa_t = pa[n_p:].reshape(db, ds, 3 * aw).transpose(1, 0, 2)
        pc_t = pc[n_p:].reshape(db, ds, 2 * cw).transpose(1, 0, 2)
        af_s, cf_s, ha_s, hc_s = _mix_sample(pa_t, pc_t, state_conv_a[l].transpose(1, 0, 2),
                                             state_conv_c[l].transpose(1, 0, 2), aw, cw, *conv_w)
        af_s = af_s.transpose(1, 0, 2).reshape(n_s, aw)
        cf_s = cf_s.transpose(1, 0, 2).reshape(n_s, cw)

        at_p = _attn_prompt(q, kvb, sb_bias[l], bp, seq, nh, hd, tqa, tq, seq // tqa, 0, BF16, False, "attn_prompt")
        q_s = q[n_p:].astype(F32).reshape(db, ds, nh // 2, 2, hd)
        zq = jnp.zeros((db, ds, nh // 2, hd), F32)
        q_bd = jnp.concatenate([jnp.concatenate([q_s[:, :, :, 0], zq], axis=-1),
                                jnp.concatenate([zq, q_s[:, :, :, 1]], axis=-1)], axis=1)
        q_bd = q_bd.transpose(0, 2, 1, 3).reshape(db, (nh // 2) * 2 * ds, 2 * hd)
        kn = kv[n_p:, :hw].reshape(db, ds, hw)
        vn = kv[n_p:, hw:].reshape(db, ds, hw)
        raw = _attn_sample(q_bd, kn, vn, cache_k, cache_v, l, page_table, sb_bias[l], ds, nh, hd)
        raw = raw.reshape(db, nh // 2, 2, ds, 2, hd)
        at_s = jnp.stack([raw[:, :, 0, :, 0], raw[:, :, 1, :, 1]], axis=2)
        at_s = at_s.transpose(0, 3, 1, 2, 4).reshape(n_s, hw).astype(BF16)

        merged = _merge((af_p, at_p, cf_p), (af_s, at_s, cf_s), gates, w_out_a[l].astype(BF16),
                        w_out_b[l].astype(BF16), w_out_c[l].astype(BF16), tmx)
        wr = jnp.zeros((d, LANES), F32).at[:, :ng].set(w_router_group[l]).at[:, ng:ng + ne].set(w_router_expert[l])
        br = jnp.zeros((1, LANES), F32).at[0, :ng].set(b_router_group[l]).at[0, ng:ng + ne].set(b_router_expert[l])
        x_prompt_in = x_parts[0][:n_p]
        x, h2, logits = _outproj(merged, x_parts, w_o[l].astype(BF16), norm_ffn[l], wr, br, tmx)

        if l < depth - 1:
            x_t = x_prompt_in.reshape(bp, seq, d)[:, seq - tq:].reshape(bp * tq, d)
            tp = lambda c0, c1, nm: _tail_proj(x_t, norm_mix[l], w_in, l, c0, c1, tn, nm)
            pa_x, q_x = tp(0, c_q, "tail_proj_a"), tp(c_q, c_k, "tail_proj_q")
            pc_x, gl_x = tp(c_c, c_g, "tail_proj_c"), tp(c_g, c_end, "tail_proj_gates")
            af_x, cf_x, _, _ = _mix_prompt(pa_x, pc_x, bp, tq, aw, cw, *conv_w, out_dtype=F32, gate_is_logit=True,
                                           name="tail_mix")
            at_x = _attn_prompt(q_x, kvb, sb_bias[l], bp, seq, nh, hd, tq, tq, 1, nq - 1, F32, True, "tail_attn")
            merged_x = _tail_merge(af_x, at_x, cf_x, gl_x, w_out_a, w_out_b, w_out_c, l, tn)
            logits_x = _tail_router(merged_x, x_t, w_o, l, norm_ffn[l], wr, br, tn)
            keep = tq - 2 * kc
            logits = logits.reshape(-1, LANES)
            for b in range(bp):
                logits = lax.dynamic_update_slice(logits, logits_x[b * tq + keep:(b + 1) * tq],
                                                  (b * seq + seq - tq + keep, 0))

        meta, counts = _route(logits, ng, ne, tmx)
        e1, e2 = meta[:, 0].astype(jnp.int32), meta[:, 1].astype(jnp.int32)
        cnt = counts[0, ng:ng + ne].astype(jnp.int32)
        ends = jnp.cumsum(cnt)
        off = ends - cnt
        pos1 = off[e1] + meta[:, 4].astype(jnp.int32)
        pos2 = off[e2] + meta[:, 5].astype(jnp.int32)
        x_sorted = _dispatch(h2, pos1, pos2)
        marks = jnp.concatenate([jnp.arange(n_tiles, dtype=jnp.int32) * tme, off])
        ids = jnp.arange(marks.shape[0], dtype=jnp.int32)
        before = (marks[None, :] < marks[:, None]) | ((marks[None, :] == marks[:, None]) & (ids[None, :] < ids[:, None]))
        rank = jnp.sum(before, axis=1).astype(jnp.int32)
        starts = jnp.sum(jnp.where(rank[None, :] == ids[:, None], marks[None, :], 0), axis=1).astype(jnp.int32)
        stops = jnp.concatenate([starts[1:], jnp.full((1,), TOP_K * n, jnp.int32)])
        item_tile = jnp.minimum(starts // tme, n_tiles - 1)
        item_expert = jnp.minimum(jnp.sum(ends[None, :] <= starts[:, None], axis=1), ne - 1).astype(jnp.int32)
        y_sorted = _experts(x_sorted, item_tile, item_expert, starts - item_tile * tme, stops - item_tile * tme,
                            w_gate, w_up, w_down, l, tme)
        if l < depth - 1:
            x, h = _combine(x, meta, y_sorted, pos1, pos2, norm_mix[l + 1], tc)
            x_parts = (x,)
        else:
            y_p, y_s = _combine(x, meta, y_sorted, pos1, pos2, norm_final, tc, split_rows=n_p)

        outs["kv"].append(kv)
        outs["ap"].append(ha_p)
        outs["as"].append(ha_s.transpose(1, 0, 2))
        outs["cp"].append(hc_p)
        outs["cs"].append(hc_s.transpose(1, 0, 2))

    k_p, v_p, k_s, v_s = _kv_out(outs["kv"], n_p, n_s, nh, hd, tmx)
    return (y_p.reshape(bp, seq, d), y_s.reshape(db, ds, d),
            k_p.reshape(depth, bp, seq, nh, hd), v_p.reshape(depth, bp, seq, nh, hd),
            k_s.reshape(depth, db, ds, nh, hd), v_s.reshape(depth, db, ds, nh, hd),
            jnp.stack(outs["ap"]), jnp.stack(outs["as"]), jnp.stack(outs["cp"]), jnp.stack(outs["cs"]))
```
